```python
import jax, jax.numpy as jnp
from jax import lax
import numpy as np

D_MODEL = 1024
BATCH = 2
SEQ = 8192
DEPTH = 1

CONV_WIDTH = 512
CONV_KERNEL = 31
N_HEADS = 8
QK_NOPE_DIM = 64
QK_ROPE_DIM = 32
V_HEAD_DIM = 64
Q_LORA_RANK = 384
KV_LORA_RANK = 256
ROPE_THETA = 10000.0
Q_BLOCK = 128
QK_HEAD_DIM = QK_NOPE_DIM + QK_ROPE_DIM
MLA_WIDTH = N_HEADS * V_HEAD_DIM
D_FF = 2816
FFN_KERNEL = 3
N_BRANCHES = 2
NORM_EPS = 1e-6
IN_SPLITS = (2 * CONV_WIDTH, Q_LORA_RANK, KV_LORA_RANK, QK_ROPE_DIM, N_BRANCHES * D_MODEL)
D_IN = 2 * CONV_WIDTH + Q_LORA_RANK + KV_LORA_RANK + QK_ROPE_DIM + N_BRANCHES * D_MODEL

kernel_name = "hybrid_conformer_mla_gated_encoder"


def rms_norm(x, g):
    xf = x.astype(jnp.float32)
    y = xf * lax.rsqrt(jnp.mean(xf * xf, axis=-1, keepdims=True) + NORM_EPS)
    return (y * g.astype(jnp.float32)).astype(x.dtype)


def layer_norm(x, g, b):
    xf = x.astype(jnp.float32)
    mu = jnp.mean(xf, axis=-1, keepdims=True)
    xc = xf - mu
    y = xc * lax.rsqrt(jnp.mean(xc * xc, axis=-1, keepdims=True) + NORM_EPS)
    return (y * g.astype(jnp.float32) + b.astype(jnp.float32)).astype(x.dtype)


def depthwise_conv(x, w, b):
    k, c = w.shape
    pad = k // 2
    y = lax.conv_general_dilated(
        x, w[:, None, :].astype(x.dtype), window_strides=(1,), padding=[(pad, pad)],
        dimension_numbers=("NWC", "WIO", "NWC"), feature_group_count=c)
    return y + b


def split_cols(t, sizes):
    out, off = [], 0
    for s in sizes:
        out.append(t[..., off:off + s])
        off += s
    return out


def rope_tables(positions):
    inv_freq = 1.0 / (ROPE_THETA ** (jnp.arange(0, QK_ROPE_DIM, 2, dtype=jnp.float32) / QK_ROPE_DIM))
    ang = positions.astype(jnp.float32)[..., None] * inv_freq
    return jnp.cos(ang), jnp.sin(ang)


def apply_rope(t, cos, sin):
    tf = t.astype(jnp.float32)
    t1, t2 = tf[..., : QK_ROPE_DIM // 2], tf[..., QK_ROPE_DIM // 2:]
    return jnp.concatenate([t1 * cos - t2 * sin, t2 * cos + t1 * sin], axis=-1).astype(t.dtype)


def conformer_conv(a, dw_w, dw_b, ln_g, ln_b, w_out, b_out):
    val, gate = a[..., :CONV_WIDTH], a[..., CONV_WIDTH:]
    u = val * jax.nn.sigmoid(gate)
    u = depthwise_conv(u, dw_w, dw_b)
    u = jax.nn.silu(layer_norm(u, ln_g, ln_b))
    return u @ w_out + b_out


def mla(q_c, kv_c, k_rope_in, positions, q_norm_g, w_uq, kv_norm_g, w_ukv, w_mla_out):
    b, s, _ = q_c.shape
    q = (rms_norm(q_c, q_norm_g) @ w_uq).reshape(b, s, N_HEADS, QK_HEAD_DIM)
    q_nope, q_rope = q[..., :QK_NOPE_DIM], q[..., QK_NOPE_DIM:]
    kv = (rms_norm(kv_c, kv_norm_g) @ w_ukv).reshape(b, s, N_HEADS, QK_NOPE_DIM + V_HEAD_DIM)
    k_nope, v = kv[..., :QK_NOPE_DIM], kv[..., QK_NOPE_DIM:]
    cos, sin = rope_tables(positions)
    q_rope = apply_rope(q_rope, cos[:, :, None, :], sin[:, :, None, :])
    k_rope = apply_rope(k_rope_in, cos, sin)
    scale = QK_HEAD_DIM ** -0.5
    n_blk = s // Q_BLOCK

    def to_blocks(t):
        return jnp.moveaxis(t.reshape(b, n_blk, Q_BLOCK, *t.shape[2:]), 1, 0)

    def attend(blk):
        qn, qr = blk
        sc = (jnp.einsum('bqhd,bkhd->bhqk', qn, k_nope)
              + jnp.einsum('bqhr,bkr->bhqk', qr, k_rope)).astype(jnp.float32) * scale
        p = jax.nn.softmax(sc, axis=-1).astype(v.dtype)
        return jnp.einsum('bhqk,bkhd->bqhd', p, v)

    o = lax.map(attend, (to_blocks(q_nope), to_blocks(q_rope)))
    o = jnp.moveaxis(o, 0, 1).reshape(b, s, MLA_WIDTH)
    return o @ w_mla_out


def conv_ffn(h, w_up, dw_w, dw_b, w_down):
    gu = depthwise_conv(h @ w_up, dw_w, dw_b)
    gate, up = gu[..., :D_FF], gu[..., D_FF:]
    return (jax.nn.silu(gate) * up) @ w_down


def setup_inputs(seed: int = 0) -> dict:
    key = jax.random.key(seed)
    ks = iter(jax.random.split(key, 32))

    def w(shape, fan_in):
        return jax.random.normal(next(ks), shape, jnp.float32) * (fan_in ** -0.5)

    def gain(shape):
        return 1.0 + 0.02 * jax.random.normal(next(ks), shape, jnp.float32)

    def bias(shape):
        return 0.02 * jax.random.normal(next(ks), shape, jnp.float32)

    L = DEPTH
    x = jax.random.normal(next(ks), (BATCH, SEQ, D_MODEL), jnp.float32)
    offset = jax.random.randint(next(ks), (BATCH, 1), 0, SEQ, dtype=jnp.int32)
    positions = offset + jnp.arange(SEQ, dtype=jnp.int32)[None, :]
    return {
        "x": x,
        "positions": positions,
        "norm1_g": gain((L, D_MODEL)),
        "w_in": w((L, D_MODEL, D_IN), D_MODEL),
        "conv_dw_w": w((L, CONV_KERNEL, CONV_WIDTH), CONV_KERNEL),
        "conv_dw_b": bias((L, CONV_WIDTH)),
        "conv_ln_g": gain((L, CONV_WIDTH)),
        "conv_ln_b": bias((L, CONV_WIDTH)),
        "w_conv_out": w((L, CONV_WIDTH, D_MODEL), CONV_WIDTH),
        "b_conv_out": bias((L, D_MODEL)),
        "q_norm_g": gain((L, Q_LORA_RANK)),
        "w_uq": w((L, Q_LORA_RANK, N_HEADS * QK_HEAD_DIM), Q_LORA_RANK),
        "kv_norm_g": gain((L, KV_LORA_RANK)),
        "w_ukv": w((L, KV_LORA_RANK, N_HEADS * (QK_NOPE_DIM + V_HEAD_DIM)), KV_LORA_RANK),
        "w_mla_out": w((L, MLA_WIDTH, D_MODEL), MLA_WIDTH),
        "w_out": w((L, D_MODEL, D_MODEL), D_MODEL),
        "norm2_g": gain((L, D_MODEL)),
        "w_ffn_up": w((L, D_MODEL, 2 * D_FF), D_MODEL),
        "ffn_dw_w": w((L, FFN_KERNEL, 2 * D_FF), FFN_KERNEL),
        "ffn_dw_b": bias((L, 2 * D_FF)),
        "w_ffn_down": w((L, D_FF, D_MODEL), D_FF),
        "norm_f_g": gain((D_MODEL,)),
    }


def reference(x, positions, norm1_g, w_in, conv_dw_w, conv_dw_b, conv_ln_g, conv_ln_b,
              w_conv_out, b_conv_out, q_norm_g, w_uq, kv_norm_g, w_ukv, w_mla_out, w_out,
              norm2_g, w_ffn_up, ffn_dw_w, ffn_dw_b, w_ffn_down, norm_f_g):
    for l in range(DEPTH):
        h = rms_norm(x, norm1_g[l])
        proj = h @ w_in[l]
        a_in, q_c, kv_c, k_rope_in, gates = split_cols(proj, IN_SPLITS)
        y_conv = conformer_conv(a_in, conv_dw_w[l], conv_dw_b[l], conv_ln_g[l], conv_ln_b[l],
                                w_conv_out[l], b_conv_out[l])
        y_mla = mla(q_c, kv_c, k_rope_in, positions, q_norm_g[l], w_uq[l],
                    kv_norm_g[l], w_ukv[l], w_mla_out[l])
        g = jax.nn.sigmoid(gates.astype(jnp.float32)).astype(x.dtype)
        merged = g[..., :D_MODEL] * y_conv + g[..., D_MODEL:] * y_mla
        x = x + merged @ w_out[l]
        h2 = rms_norm(x, norm2_g[l])
        x = x + conv_ffn(h2, w_ffn_up[l], ffn_dw_w[l], ffn_dw_b[l], w_ffn_down[l])
    return rms_norm(x, norm_f_g)
```

```python
import functools
import math

import jax
import jax.numpy as jnp
from jax import lax
from jax.experimental import pallas as pl
from jax.experimental.pallas import tpu as pltpu

D_MODEL = 1024
CONV_WIDTH = 512
CONV_KERNEL = 31
N_HEADS = 8
QK_NOPE_DIM = 64
QK_ROPE_DIM = 32
V_HEAD_DIM = 64
Q_LORA_RANK = 384
KV_LORA_RANK = 256
ROPE_THETA = 10000.0
QK_HEAD_DIM = QK_NOPE_DIM + QK_ROPE_DIM
MLA_WIDTH = N_HEADS * V_HEAD_DIM
D_FF = 2816
FFN_KERNEL = 3
NORM_EPS = 1e-6

LANES = 128
HALF_ROPE = QK_ROPE_DIM // 2
K_SLOT = LANES
HEADS_PER_STEP = 2
VMEM_LIMIT = 56 * 1024 * 1024

TM = 512
CONV_HALO = 16
CONV_ROWS = 64
FFN_HALO = 16
FFN_CHUNK = 256
TQ = 512
TQ_SUB = 256
TK = 512

_NT = (((1,), (1,)), ((), ()))


def _rms(x, g):
    return x * lax.rsqrt(jnp.mean(x * x, axis=-1, keepdims=True) + NORM_EPS) * g


def _dot(a, b):
    return jnp.dot(a, b, preferred_element_type=jnp.float32)


def _dot_nt(a, b):
    return lax.dot_general(a, b, _NT, preferred_element_type=jnp.float32)


def _resident(shape):
    nd = len(shape)
    return pl.BlockSpec(shape, lambda *_: (0,) * nd, pipeline_mode=pl.Buffered(1))


def _proj_kernel(x_ref, pos_ref, g1_ref, wa_ref, wq_ref, wkv_ref, wkrT_ref, wg_ref,
                 qng_ref, kvng_ref, wuqT_ref, wukp_ref, wuvT_ref, invf_ref,
                 u_ref, g_ref, qT_ref, k_ref, vT_ref, *, q_scale):
    bf = jnp.bfloat16
    hb = _rms(x_ref[...], g1_ref[...]).astype(bf)

    a = _dot(hb, wa_ref[...])
    u_ref[...] = a[:, :CONV_WIDTH] * jax.nn.sigmoid(a[:, CONV_WIDTH:])
    g_ref[...] = jax.nn.sigmoid(_dot(hb, wg_ref[...])).astype(bf)

    qn = _rms(_dot(hb, wq_ref[...]), qng_ref[...]).astype(bf)
    kvn = _rms(_dot(hb, wkv_ref[...]), kvng_ref[...]).astype(bf)

    ang = invf_ref[...] * pos_ref[...].astype(jnp.float32)
    cos_t, sin_t = jnp.cos(ang), jnp.sin(ang)

    def rope_t(t1, t2):
        return t1 * cos_t - t2 * sin_t, t2 * cos_t + t1 * sin_t

    q_t = _dot_nt(wuqT_ref[...], qn)
    for h in range(N_HEADS):
        r0 = h * QK_HEAD_DIM
        r1, r2 = rope_t(q_t[r0 + QK_NOPE_DIM:r0 + QK_NOPE_DIM + HALF_ROPE],
                        q_t[r0 + QK_NOPE_DIM + HALF_ROPE:r0 + QK_HEAD_DIM])
        blk = jnp.concatenate([q_t[r0:r0 + QK_NOPE_DIM], r1, r2], axis=0) * q_scale
        qT_ref[r0:r0 + QK_HEAD_DIM, :] = blk.astype(bf)

    kr_t = _dot_nt(wkrT_ref[...], hb)
    k1, k2 = rope_t(kr_t[:HALF_ROPE], kr_t[HALF_ROPE:])
    tm = kr_t.shape[1]
    kr_slot_t = jnp.concatenate(
        [jnp.zeros((QK_NOPE_DIM, tm), jnp.float32), k1, k2,
         jnp.zeros((K_SLOT - QK_HEAD_DIM, tm), jnp.float32)], axis=0)
    kr_slot = kr_slot_t.T
    knp = _dot(kvn, wukp_ref[...])
    for h in range(N_HEADS):
        c0 = h * K_SLOT
        k_ref[:, c0:c0 + K_SLOT] = (knp[:, c0:c0 + K_SLOT] + kr_slot).astype(bf)

    vT_ref[...] = _dot_nt(wuvT_ref[...], kvn).astype(bf)


def _conv_kernel(u_ref, up_ref, un_ref, g_ref, dww_ref, dwb_ref, lng_ref, lnb_ref,
                 wo_ref, bo_ref, c_ref, ext_ref, act_ref):
    i = pl.program_id(1)
    n = pl.num_programs(1)
    tm = u_ref.shape[0]
    pad = CONV_KERNEL // 2
    ext_ref[0:CONV_HALO, :] = jnp.where(i > 0, up_ref[...], 0.0)
    ext_ref[CONV_HALO:CONV_HALO + tm, :] = u_ref[...]
    ext_ref[CONV_HALO + tm:, :] = jnp.where(i < n - 1, un_ref[...], 0.0)

    for r0 in range(0, tm, CONV_ROWS):
        acc = jnp.zeros((CONV_ROWS, CONV_WIDTH), jnp.float32)
        for k in range(CONV_KERNEL):
            s = CONV_HALO + r0 + k - pad
            acc = acc + ext_ref[s:s + CONV_ROWS, :] * dww_ref[k:k + 1, :]
        y = acc + dwb_ref[...]
        mu = jnp.mean(y, axis=-1, keepdims=True)
        yc = y - mu
        yn = yc * lax.rsqrt(jnp.mean(yc * yc, axis=-1, keepdims=True) + NORM_EPS)
        yn = yn * lng_ref[...] + lnb_ref[...]
        act_ref[r0:r0 + CONV_ROWS, :] = (yn * jax.nn.sigmoid(yn)).astype(jnp.bfloat16)

    y_conv = _dot(act_ref[...], wo_ref[...]) + bo_ref[...]
    c_ref[...] = (g_ref[...].astype(jnp.float32) * y_conv).astype(jnp.bfloat16)


def _attn_kernel(qT_ref, k_ref, vT_ref, o_ref):
    bf = jnp.bfloat16
    seq = k_ref.shape[0]
    tq = qT_ref.shape[1]
    n_chunks = seq // TK
    ones_rows = jnp.ones((16, TK), bf)
    pad_rows = jnp.zeros((K_SLOT - QK_HEAD_DIM, TQ_SUB), bf)

    for qs in range(tq // TQ_SUB):
        outs = []
        for h in range(HEADS_PER_STEP):
            q_t = jnp.concatenate(
                [qT_ref[h * QK_HEAD_DIM:(h + 1) * QK_HEAD_DIM, qs * TQ_SUB:(qs + 1) * TQ_SUB], pad_rows],
                axis=0)

            def body(j, carry, h=h, q_t=q_t):
                m, acc = carry
                r0 = pl.multiple_of(j * TK, TK)
                k_blk = k_ref[pl.ds(r0, TK), h * K_SLOT:(h + 1) * K_SLOT]
                s_t = _dot(k_blk, q_t)
                m_new = jnp.maximum(m, jnp.max(s_t, axis=0, keepdims=True))
                p_t = jnp.exp2(s_t - m_new).astype(bf)
                v_ext = jnp.concatenate(
                    [vT_ref[h * V_HEAD_DIM:(h + 1) * V_HEAD_DIM, pl.ds(r0, TK)], ones_rows], axis=0)
                acc = jnp.exp2(m - m_new) * acc + _dot(v_ext, p_t)
                return m_new, acc

            m0 = jnp.full((1, TQ_SUB), -jnp.inf, jnp.float32)
            acc0 = jnp.zeros((V_HEAD_DIM + 16, TQ_SUB), jnp.float32)
            _, acc = lax.fori_loop(0, n_chunks, body, (m0, acc0), unroll=2)
            outs.append(acc[:V_HEAD_DIM] / acc[V_HEAD_DIM:V_HEAD_DIM + 1])
        o_t = jnp.concatenate(outs, axis=0)
        o_ref[qs * TQ_SUB:(qs + 1) * TQ_SUB, :] = o_t.T.astype(bf)


def _merge_kernel(o_ref, c_ref, g_ref, x_ref, wm_ref, wo_ref, n2_ref, x1_ref, h2_ref):
    y_mla = _dot(o_ref[...], wm_ref[...])
    merged = c_ref[...].astype(jnp.float32) + g_ref[...].astype(jnp.float32) * y_mla
    x1 = x_ref[...] + _dot(merged.astype(jnp.bfloat16), wo_ref[...])
    x1_ref[...] = x1
    h2_ref[...] = _rms(x1, n2_ref[...]).astype(jnp.bfloat16)


def _ffn_kernel(h_ref, hp_ref, hn_ref, x1_ref, wup_ref, dww_ref, dwb_ref, wdn_ref, nf_ref,
                out_ref, acc_ref):
    i = pl.program_id(1)
    n = pl.num_programs(1)
    tm = h_ref.shape[0]
    hp = jnp.where(i > 0, hp_ref[...], jnp.zeros_like(hp_ref))
    hn = jnp.where(i < n - 1, hn_ref[...], jnp.zeros_like(hn_ref))
    h_ext = jnp.concatenate([hp, h_ref[...], hn], axis=0)
    rows = tm + 2 * FFN_HALO

    def conv3(z, c0):
        w = dww_ref[:, c0:c0 + FFN_CHUNK]
        zp = pltpu.roll(z, 1, 0)[FFN_HALO:FFN_HALO + tm]
        zn = pltpu.roll(z, rows - 1, 0)[FFN_HALO:FFN_HALO + tm]
        return (zp * w[0:1] + z[FFN_HALO:FFN_HALO + tm] * w[1:2] + zn * w[2:3]
                + dwb_ref[:, c0:c0 + FFN_CHUNK])

    for c in range(D_FF // FFN_CHUNK):
        cg, cu = c * FFN_CHUNK, D_FF + c * FFN_CHUNK
        gate = conv3(_dot(h_ext, wup_ref[:, cg:cg + FFN_CHUNK]), cg)
        up = conv3(_dot(h_ext, wup_ref[:, cu:cu + FFN_CHUNK]), cu)
        act = (gate * jax.nn.sigmoid(gate) * up).astype(jnp.bfloat16)
        part = _dot(act, wdn_ref[c * FFN_CHUNK:(c + 1) * FFN_CHUNK, :])
        if c == 0:
            acc_ref[...] = part
        else:
            acc_ref[...] += part

    out_ref[...] = _rms(x1_ref[...] + acc_ref[...], nf_ref[...])


def _layer(x, positions, norm1_g, w_in, conv_dw_w, conv_dw_b, conv_ln_g, conv_ln_b,
           w_conv_out, b_conv_out, q_norm_g, w_uq, kv_norm_g, w_ukv, w_mla_out, w_out,
           norm2_g, w_ffn_up, ffn_dw_w, ffn_dw_b, w_ffn_down, norm_f_g):
    bf = jnp.bfloat16
    f32 = jnp.float32
    B, S, D = x.shape
    assert D == D_MODEL and S % TM == 0 and S % TQ == 0 and S % TK == 0
    n_t = S // TM
    row = lambda v: v.reshape(1, -1).astype(f32)
    params = functools.partial(pltpu.CompilerParams, vmem_limit_bytes=VMEM_LIMIT)

    o_q = 2 * CONV_WIDTH
    o_kv = o_q + Q_LORA_RANK
    o_kr = o_kv + KV_LORA_RANK
    o_g = o_kr + QK_ROPE_DIM
    wa = w_in[:, :o_q].astype(bf)
    wq = w_in[:, o_q:o_kv].astype(bf)
    wkv = w_in[:, o_kv:o_kr].astype(bf)
    wkrT = w_in[:, o_kr:o_g].T.astype(bf)
    wg = w_in[:, o_g:].astype(bf)
    wuqT = w_uq.T.astype(bf)
    w_ukv_h = w_ukv.reshape(KV_LORA_RANK, N_HEADS, QK_NOPE_DIM + V_HEAD_DIM)
    wukp = jnp.pad(w_ukv_h[:, :, :QK_NOPE_DIM], ((0, 0), (0, 0), (0, K_SLOT - QK_NOPE_DIM)))
    wukp = wukp.reshape(KV_LORA_RANK, N_HEADS * K_SLOT).astype(bf)
    wuvT = w_ukv_h[:, :, QK_NOPE_DIM:].reshape(KV_LORA_RANK, MLA_WIDTH).T.astype(bf)
    inv_freq = 1.0 / (ROPE_THETA ** (jnp.arange(0, QK_ROPE_DIM, 2, dtype=f32) / QK_ROPE_DIM))
    q_scale = (QK_HEAD_DIM ** -0.5) * math.log2(math.e)

    tok = lambda w: pl.BlockSpec((None, TM, w), lambda b, i: (b, i, 0))
    tok_t = lambda r: pl.BlockSpec((None, r, TM), lambda b, i: (b, 0, i))

    u, g, q_t, k, v_t = pl.pallas_call(
        functools.partial(_proj_kernel, q_scale=q_scale),
        grid=(B, n_t),
        in_specs=[tok(D), pl.BlockSpec((None, 1, TM), lambda b, i: (b, 0, i)),
                  _resident((1, D)), _resident(wa.shape), _resident(wq.shape), _resident(wkv.shape),
                  _resident(wkrT.shape), _resident(wg.shape), _resident((1, Q_LORA_RANK)),
                  _resident((1, KV_LORA_RANK)), _resident(wuqT.shape), _resident(wukp.shape),
                  _resident(wuvT.shape), _resident((HALF_ROPE, 1))],
        out_specs=[tok(CONV_WIDTH), tok(2 * D), tok_t(N_HEADS * QK_HEAD_DIM),
                   tok(N_HEADS * K_SLOT), tok_t(MLA_WIDTH)],
        out_shape=[jax.ShapeDtypeStruct((B, S, CONV_WIDTH), f32),
                   jax.ShapeDtypeStruct((B, S, 2 * D), bf),
                   jax.ShapeDtypeStruct((B, N_HEADS * QK_HEAD_DIM, S), bf),
                   jax.ShapeDtypeStruct((B, S, N_HEADS * K_SLOT), bf),
                   jax.ShapeDtypeStruct((B, MLA_WIDTH, S), bf)],
        compiler_params=params(dimension_semantics=("arbitrary", "arbitrary")),
        name="proj",
    )(x, positions.reshape(B, 1, S), row(norm1_g), wa, wq, wkv, wkrT, wg, row(q_norm_g),
      row(kv_norm_g), wuqT, wukp, wuvT, inv_freq.reshape(HALF_ROPE, 1))

    hb = TM // CONV_HALO
    n_hb = S // CONV_HALO
    c = pl.pallas_call(
        _conv_kernel,
        grid=(B, n_t),
        in_specs=[tok(CONV_WIDTH),
                  pl.BlockSpec((None, CONV_HALO, CONV_WIDTH),
                               lambda b, i: (b, jnp.maximum(i * hb - 1, 0), 0)),
                  pl.BlockSpec((None, CONV_HALO, CONV_WIDTH),
                               lambda b, i: (b, jnp.minimum((i + 1) * hb, n_hb - 1), 0)),
                  tok(D),
                  _resident((CONV_KERNEL, CONV_WIDTH)), _resident((1, CONV_WIDTH)),
                  _resident((1, CONV_WIDTH)), _resident((1, CONV_WIDTH)),
                  _resident((CONV_WIDTH, D)), _resident((1, D))],
        out_specs=tok(D),
        out_shape=jax.ShapeDtypeStruct((B, S, D), bf),
        scratch_shapes=[pltpu.VMEM((TM + 2 * CONV_HALO, CONV_WIDTH), f32),
                        pltpu.VMEM((TM, CONV_WIDTH), bf)],
        compiler_params=params(dimension_semantics=("arbitrary", "arbitrary")),
        name="conv",
    )(u, u, u, g, conv_dw_w.astype(f32), row(conv_dw_b), row(conv_ln_g), row(conv_ln_b),
      w_conv_out.astype(bf), row(b_conv_out))

    n_hp = N_HEADS // HEADS_PER_STEP
    o = pl.pallas_call(
        _attn_kernel,
        grid=(B, n_hp, S // TQ),
        in_specs=[pl.BlockSpec((None, HEADS_PER_STEP * QK_HEAD_DIM, TQ), lambda b, h, i: (b, h, i)),
                  pl.BlockSpec((None, S, HEADS_PER_STEP * K_SLOT), lambda b, h, i: (b, 0, h)),
                  pl.BlockSpec((None, HEADS_PER_STEP * V_HEAD_DIM, S), lambda b, h, i: (b, h, 0))],
        out_specs=pl.BlockSpec((None, TQ, HEADS_PER_STEP * V_HEAD_DIM), lambda b, h, i: (b, i, h)),
        out_shape=jax.ShapeDtypeStruct((B, S, MLA_WIDTH), bf),
        compiler_params=params(dimension_semantics=("arbitrary", "arbitrary", "arbitrary")),
        name="attn",
    )(q_t, k, v_t)

    x1, h2 = pl.pallas_call(
        _merge_kernel,
        grid=(B, n_t),
        in_specs=[tok(MLA_WIDTH), tok(D),
                  pl.BlockSpec((None, TM, D), lambda b, i: (b, i, 1)),
                  tok(D), _resident((MLA_WIDTH, D)), _resident((D, D)), _resident((1, D))],
        out_specs=[tok(D), tok(D)],
        out_shape=[jax.ShapeDtypeStruct((B, S, D), f32), jax.ShapeDtypeStruct((B, S, D), bf)],
        compiler_params=params(dimension_semantics=("arbitrary", "arbitrary")),
        name="merge",
    )(o, c, g, x, w_mla_out.astype(bf), w_out.astype(bf), row(norm2_g))

    fb = TM // FFN_HALO
    n_fb = S // FFN_HALO
    out = pl.pallas_call(
        _ffn_kernel,
        grid=(B, n_t),
        in_specs=[tok(D),
                  pl.BlockSpec((None, FFN_HALO, D), lambda b, i: (b, jnp.maximum(i * fb - 1, 0), 0)),
                  pl.BlockSpec((None, FFN_HALO, D),
                               lambda b, i: (b, jnp.minimum((i + 1) * fb, n_fb - 1), 0)),
                  tok(D), _resident((D, 2 * D_FF)), _resident((FFN_KERNEL, 2 * D_FF)),
                  _resident((1, 2 * D_FF)), _resident((D_FF, D)), _resident((1, D))],
        out_specs=tok(D),
        out_shape=jax.ShapeDtypeStruct((B, S, D), f32),
        scratch_shapes=[pltpu.VMEM((TM, D), f32)],
        compiler_params=params(dimension_semantics=("arbitrary", "arbitrary")),
        name="ffn",
    )(h2, h2, h2, x1, w_ffn_up.astype(bf), ffn_dw_w.astype(f32), row(ffn_dw_b),
      w_ffn_down.astype(bf), row(norm_f_g))
    return out


def kernel(x, positions, norm1_g, w_in, conv_dw_w, conv_dw_b, conv_ln_g, conv_ln_b, w_conv_out,
           b_conv_out, q_norm_g, w_uq, kv_norm_g, w_ukv, w_mla_out, w_out, norm2_g, w_ffn_up,
           ffn_dw_w, ffn_dw_b, w_ffn_down, norm_f_g):
    assert norm1_g.shape[0] == 1, "single-layer block"
    return _layer(x, positions, norm1_g[0], w_in[0], conv_dw_w[0], conv_dw_b[0], conv_ln_g[0],
                  conv_ln_b[0], w_conv_out[0], b_conv_out[0], q_norm_g[0], w_uq[0], kv_norm_g[0],
                  w_ukv[0], w_mla_out[0], w_out[0], norm2_g[0], w_ffn_up[0], ffn_dw_w[0],
                  ffn_dw_b[0], w_ffn_down[0], norm_f_g)
```

```python
import functools
import math

import jax
import jax.numpy as jnp
from jax import lax
from jax.experimental import pallas as pl
from jax.experimental.pallas import tpu as pltpu

D_MODEL = 1024
CONV_WIDTH = 512
CONV_KERNEL = 31
N_HEADS = 8
QK_NOPE_DIM = 64
QK_ROPE_DIM = 32
V_HEAD_DIM = 64
Q_LORA_RANK = 384
KV_LORA_RANK = 256
ROPE_THETA = 10000.0
QK_HEAD_DIM = QK_NOPE_DIM + QK_ROPE_DIM
MLA_WIDTH = N_HEADS * V_HEAD_DIM
D_FF = 2816
FFN_KERNEL = 3
NORM_EPS = 1e-6

LANES = 128
HALF_ROPE = QK_ROPE_DIM // 2
K_SLOT = LANES
HEADS_PER_STEP = 2
VMEM_LIMIT = 56 * 1024 * 1024

TM = 512
CONV_HALO = 16
CONV_ROWS = 128
FFN_HALO = 16
FFN_CHUNK = 256
TQ = 512
TQ_SUB = 256
TK = 512

_NT = (((1,), (1,)), ((), ()))


def _rms(x, g):
    return x * lax.rsqrt(jnp.mean(x * x, axis=-1, keepdims=True) + NORM_EPS) * g


def _dot(a, b):
    return jnp.dot(a, b, preferred_element_type=jnp.float32)


def _dot_nt(a, b):
    return lax.dot_general(a, b, _NT, preferred_element_type=jnp.float32)


def _resident(shape):
    nd = len(shape)
    return pl.BlockSpec(shape, lambda *_: (0,) * nd, pipeline_mode=pl.Buffered(1))


def _proj_kernel(x_ref, pos_ref, g1_ref, wa_ref, wq_ref, wkv_ref, wkrT_ref, wg_ref,
                 qng_ref, kvng_ref, wuqT_ref, wukp_ref, wuvT_ref, invf_ref,
                 u_ref, g_ref, qT_ref, k_ref, vT_ref, *, q_scale):
    bf = jnp.bfloat16
    hb = _rms(x_ref[...], g1_ref[...]).astype(bf)

    a = _dot(hb, wa_ref[...])
    u_ref[...] = a[:, :CONV_WIDTH] * jax.nn.sigmoid(a[:, CONV_WIDTH:])
    g_ref[...] = jax.nn.sigmoid(_dot(hb, wg_ref[...])).astype(bf)

    qn = _rms(_dot(hb, wq_ref[...]), qng_ref[...]).astype(bf)
    kvn = _rms(_dot(hb, wkv_ref[...]), kvng_ref[...]).astype(bf)

    ang = invf_ref[...] * pos_ref[...].astype(jnp.float32)
    cos_t, sin_t = jnp.cos(ang), jnp.sin(ang)

    def rope_t(t1, t2):
        return t1 * cos_t - t2 * sin_t, t2 * cos_t + t1 * sin_t

    q_t = _dot_nt(wuqT_ref[...], qn)
    tm = q_t.shape[1]
    slot_pad = jnp.zeros((K_SLOT - QK_HEAD_DIM, tm), jnp.float32)
    for h in range(N_HEADS):
        r0 = h * QK_HEAD_DIM
        r1, r2 = rope_t(q_t[r0 + QK_NOPE_DIM:r0 + QK_NOPE_DIM + HALF_ROPE],
                        q_t[r0 + QK_NOPE_DIM + HALF_ROPE:r0 + QK_HEAD_DIM])
        blk = jnp.concatenate([q_t[r0:r0 + QK_NOPE_DIM] * q_scale, r1 * q_scale, r2 * q_scale, slot_pad],
                              axis=0)
        qT_ref[h * K_SLOT:(h + 1) * K_SLOT, :] = blk.astype(bf)

    kr_t = _dot_nt(wkrT_ref[...], hb)
    k1, k2 = rope_t(kr_t[:HALF_ROPE], kr_t[HALF_ROPE:])
    kr_slot_t = jnp.concatenate(
        [jnp.zeros((QK_NOPE_DIM, tm), jnp.float32), k1, k2,
         jnp.zeros((K_SLOT - QK_HEAD_DIM, tm), jnp.float32)], axis=0)
    kr_slot = kr_slot_t.T
    knp = _dot(kvn, wukp_ref[...])
    for h in range(N_HEADS):
        c0 = h * K_SLOT
        k_ref[:, c0:c0 + K_SLOT] = (knp[:, c0:c0 + K_SLOT] + kr_slot).astype(bf)

    vT_ref[...] = _dot_nt(wuvT_ref[...], kvn).astype(bf)


def _conv_kernel(u_ref, up_ref, un_ref, g_ref, dww_ref, dwb_ref, lng_ref, lnb_ref,
                 wo_ref, bo_ref, c_ref, ext_ref, act_ref):
    i = pl.program_id(1)
    n = pl.num_programs(1)
    tm = u_ref.shape[0]
    n_lt = CONV_WIDTH // LANES
    lane = lambda l: slice(l * LANES, (l + 1) * LANES)
    for l in range(n_lt):
        ext_ref[l, 0:CONV_HALO, :] = jnp.where(i > 0, up_ref[:, lane(l)], 0.0)
        ext_ref[l, CONV_HALO:CONV_HALO + tm, :] = u_ref[:, lane(l)]
        ext_ref[l, CONV_HALO + tm:, :] = jnp.where(i < n - 1, un_ref[:, lane(l)], 0.0)

    half = CONV_ROWS // 2
    first = CONV_HALO - CONV_KERNEL // 2

    def rows_chunk(ci, carry):
        r0 = ci * CONV_ROWS
        for par in range(2):
            ys = []
            for l in range(n_lt):
                acc = jnp.zeros((half, LANES), jnp.float32)
                for k in range(CONV_KERNEL):
                    tap = ext_ref[l, pl.ds(first + r0 + par + k, half, stride=2), :]
                    acc = acc + tap * dww_ref[k:k + 1, lane(l)]
                ys.append(acc)
            y = jnp.concatenate(ys, axis=1) + dwb_ref[...]
            mu = jnp.mean(y, axis=-1, keepdims=True)
            yc = y - mu
            yn = yc * lax.rsqrt(jnp.mean(yc * yc, axis=-1, keepdims=True) + NORM_EPS)
            yn = yn * lng_ref[...] + lnb_ref[...]
            a = yn * jax.nn.sigmoid(yn)
            for l in range(n_lt):
                act_ref[l, pl.ds(r0 + par, half, stride=2), :] = a[:, lane(l)]
        return carry

    lax.fori_loop(0, tm // CONV_ROWS, rows_chunk, 0)
    act = jnp.concatenate([act_ref[l] for l in range(n_lt)], axis=1).astype(jnp.bfloat16)
    y_conv = _dot(act, wo_ref[...]) + bo_ref[...]
    c_ref[...] = (g_ref[...].astype(jnp.float32) * y_conv).astype(jnp.bfloat16)


def _attn_kernel(qT_ref, k_ref, vT_ref, o_ref, s_buf, acc_ref):
    bf = jnp.bfloat16
    seq = k_ref.shape[0]
    tq = qT_ref.shape[1]
    n_chunks = seq // TK
    n_sub = tq // TQ_SUB
    chains = [(h, qs) for h in range(HEADS_PER_STEP) for qs in range(n_sub)]
    ones_rows = jnp.ones((16, TK), bf)

    def scores(j, slot):
        r0 = pl.multiple_of(j * TK, TK)
        for c, (h, qs) in enumerate(chains):
            k_blk = k_ref[pl.ds(r0, TK), h * K_SLOT:(h + 1) * K_SLOT]
            q_t = qT_ref[h * K_SLOT:(h + 1) * K_SLOT, qs * TQ_SUB:(qs + 1) * TQ_SUB]
            s_buf[slot, c] = _dot(k_blk, q_t)

    def consume(j, slot, ms):
        r0 = pl.multiple_of(j * TK, TK)
        out = []
        for c, (h, qs) in enumerate(chains):
            m_new = jnp.maximum(ms[c], jnp.max(s_buf[slot, c], axis=0, keepdims=True))
            p_t = jnp.exp2(s_buf[slot, c] - m_new).astype(bf)
            v_ext = jnp.concatenate(
                [vT_ref[h * V_HEAD_DIM:(h + 1) * V_HEAD_DIM, pl.ds(r0, TK)], ones_rows], axis=0)
            acc_ref[c] = jnp.exp2(ms[c] - m_new) * acc_ref[c] + _dot(v_ext, p_t)
            out.append(m_new)
        return tuple(out)

    acc_ref[...] = jnp.zeros_like(acc_ref)
    ms = tuple(jnp.full((1, TQ_SUB), -jnp.inf, jnp.float32) for _ in chains)
    scores(0, 0)

    def pair(i, ms):
        scores(2 * i + 1, 1)
        ms = consume(2 * i, 0, ms)
        scores(2 * i + 2, 0)
        return consume(2 * i + 1, 1, ms)

    ms = lax.fori_loop(0, n_chunks // 2 - 1, pair, ms)
    scores(n_chunks - 1, 1)
    ms = consume(n_chunks - 2, 0, ms)
    consume(n_chunks - 1, 1, ms)

    for qs in range(n_sub):
        o_t = jnp.concatenate(
            [acc_ref[h * n_sub + qs, :V_HEAD_DIM] / acc_ref[h * n_sub + qs, V_HEAD_DIM:V_HEAD_DIM + 1]
             for h in range(HEADS_PER_STEP)], axis=0)
        o_ref[qs * TQ_SUB:(qs + 1) * TQ_SUB, :] = o_t.T.astype(bf)


def _merge_kernel(o_ref, c_ref, g_ref, x_ref, wm_ref, wo_ref, n2_ref, x1_ref, h2_ref):
    y_mla = _dot(o_ref[...], wm_ref[...])
    merged = c_ref[...].astype(jnp.float32) + g_ref[...].astype(jnp.float32) * y_mla
    x1 = x_ref[...] + _dot(merged.astype(jnp.bfloat16), wo_ref[...])
    x1_ref[...] = x1
    h2_ref[...] = _rms(x1, n2_ref[...]).astype(jnp.bfloat16)


def _ffn_kernel(h_ref, hp_ref, hn_ref, x1_ref, wup_ref, dww_ref, dwb_ref, wdn_ref, nf_ref,
                out_ref, hext_ref, z_buf, act_buf, acc_ref):
    i = pl.program_id(1)
    n = pl.num_programs(1)
    tm = h_ref.shape[0]
    rows = tm + 2 * FFN_HALO
    n_c = D_FF // FFN_CHUNK
    main = slice(FFN_HALO, FFN_HALO + tm)
    hext_ref[0:FFN_HALO, :] = jnp.where(i > 0, hp_ref[...], jnp.zeros_like(hp_ref))
    hext_ref[main, :] = h_ref[...]
    hext_ref[FFN_HALO + tm:, :] = jnp.where(i < n - 1, hn_ref[...], jnp.zeros_like(hn_ref))

    def span(c, width):
        start = c * width
        return pl.ds(start if isinstance(c, int) else pl.multiple_of(start, width), width)

    cols = lambda c: span(c, 2 * FFN_CHUNK)

    def up_proj(c, slot):
        z_buf[slot] = _dot(hext_ref[...], wup_ref[:, cols(c)])

    def activate(c, slot):
        z = z_buf[slot]
        w = dww_ref[:, cols(c)]
        y = (pltpu.roll(z, 1, 0)[main] * w[0:1] + z[main] * w[1:2]
             + pltpu.roll(z, rows - 1, 0)[main] * w[2:3] + dwb_ref[:, cols(c)])
        gate, up = y[:, :FFN_CHUNK], y[:, FFN_CHUNK:]
        act_buf[slot] = (gate * jax.nn.sigmoid(gate) * up).astype(jnp.bfloat16)

    def down_proj(c, slot):
        acc_ref[...] += _dot(act_buf[slot], wdn_ref[span(c, FFN_CHUNK), :])

    acc_ref[...] = x1_ref[...]
    up_proj(0, 0)
    up_proj(1, 1)
    activate(0, 0)
    for c in range(n_c):
        if c + 2 < n_c:
            up_proj(c + 2, c % 2)
        if c + 1 < n_c:
            activate(c + 1, (c + 1) % 2)
        down_proj(c, c % 2)
    out_ref[...] = _rms(acc_ref[...], nf_ref[...])


def _layer(x, positions, norm1_g, w_in, conv_dw_w, conv_dw_b, conv_ln_g, conv_ln_b,
           w_conv_out, b_conv_out, q_norm_g, w_uq, kv_norm_g, w_ukv, w_mla_out, w_out,
           norm2_g, w_ffn_up, ffn_dw_w, ffn_dw_b, w_ffn_down, norm_f_g):
    bf = jnp.bfloat16
    f32 = jnp.float32
    B, S, D = x.shape
    assert D == D_MODEL and S % TM == 0 and S % TQ == 0 and S % TK == 0
    n_t = S // TM
    row = lambda v: v.reshape(1, -1).astype(f32)
    params = functools.partial(pltpu.CompilerParams, vmem_limit_bytes=VMEM_LIMIT)

    o_q = 2 * CONV_WIDTH
    o_kv = o_q + Q_LORA_RANK
    o_kr = o_kv + KV_LORA_RANK
    o_g = o_kr + QK_ROPE_DIM
    wa = w_in[:, :o_q].astype(bf)
    wq = w_in[:, o_q:o_kv].astype(bf)
    wkv = w_in[:, o_kv:o_kr].astype(bf)
    wkrT = w_in[:, o_kr:o_g].T.astype(bf)
    wg = w_in[:, o_g:].astype(bf)
    wuqT = w_uq.T.astype(bf)
    w_ukv_h = w_ukv.reshape(KV_LORA_RANK, N_HEADS, QK_NOPE_DIM + V_HEAD_DIM)
    wukp = jnp.pad(w_ukv_h[:, :, :QK_NOPE_DIM], ((0, 0), (0, 0), (0, K_SLOT - QK_NOPE_DIM)))
    wukp = wukp.reshape(KV_LORA_RANK, N_HEADS * K_SLOT).astype(bf)
    wuvT = w_ukv_h[:, :, QK_NOPE_DIM:].reshape(KV_LORA_RANK, MLA_WIDTH).T.astype(bf)
    inv_freq = 1.0 / (ROPE_THETA ** (jnp.arange(0, QK_ROPE_DIM, 2, dtype=f32) / QK_ROPE_DIM))
    q_scale = (QK_HEAD_DIM ** -0.5) * math.log2(math.e)

    tok = lambda w: pl.BlockSpec((None, TM, w), lambda b, i: (b, i, 0))
    tok_t = lambda r: pl.BlockSpec((None, r, TM), lambda b, i: (b, 0, i))

    u, g, q_t, k, v_t = pl.pallas_call(
        functools.partial(_proj_kernel, q_scale=q_scale),
        grid=(B, n_t),
        in_specs=[tok(D), pl.BlockSpec((None, 1, TM), lambda b, i: (b, 0, i)),
                  _resident((1, D)), _resident(wa.shape), _resident(wq.shape), _resident(wkv.shape),
                  _resident(wkrT.shape), _resident(wg.shape), _resident((1, Q_LORA_RANK)),
                  _resident((1, KV_LORA_RANK)), _resident(wuqT.shape), _resident(wukp.shape),
                  _resident(wuvT.shape), _resident((HALF_ROPE, 1))],
        out_specs=[tok(CONV_WIDTH), tok(2 * D), tok_t(N_HEADS * K_SLOT),
                   tok(N_HEADS * K_SLOT), tok_t(MLA_WIDTH)],
        out_shape=[jax.ShapeDtypeStruct((B, S, CONV_WIDTH), f32),
                   jax.ShapeDtypeStruct((B, S, 2 * D), bf),
                   jax.ShapeDtypeStruct((B, N_HEADS * K_SLOT, S), bf),
                   jax.ShapeDtypeStruct((B, S, N_HEADS * K_SLOT), bf),
                   jax.ShapeDtypeStruct((B, MLA_WIDTH, S), bf)],
        compiler_params=params(dimension_semantics=("arbitrary", "arbitrary")),
        name="proj",
    )(x, positions.reshape(B, 1, S), row(norm1_g), wa, wq, wkv, wkrT, wg, row(q_norm_g),
      row(kv_norm_g), wuqT, wukp, wuvT, inv_freq.reshape(HALF_ROPE, 1))

    hb = TM // CONV_HALO
    n_hb = S // CONV_HALO
    c = pl.pallas_call(
        _conv_kernel,
        grid=(B, n_t),
        in_specs=[tok(CONV_WIDTH),
                  pl.BlockSpec((None, CONV_HALO, CONV_WIDTH),
                               lambda b, i: (b, jnp.maximum(i * hb - 1, 0), 0)),
                  pl.BlockSpec((None, CONV_HALO, CONV_WIDTH),
                               lambda b, i: (b, jnp.minimum((i + 1) * hb, n_hb - 1), 0)),
                  tok(D),
                  _resident((CONV_KERNEL, CONV_WIDTH)), _resident((1, CONV_WIDTH)),
                  _resident((1, CONV_WIDTH)), _resident((1, CONV_WIDTH)),
                  _resident((CONV_WIDTH, D)), _resident((1, D))],
        out_specs=tok(D),
        out_shape=jax.ShapeDtypeStruct((B, S, D), bf),
        scratch_shapes=[pltpu.VMEM((CONV_WIDTH // LANES, TM + 2 * CONV_HALO, LANES), f32),
                        pltpu.VMEM((CONV_WIDTH // LANES, TM, LANES), f32)],
        compiler_params=params(dimension_semantics=("arbitrary", "arbitrary")),
        name="conv",
    )(u, u, u, g, conv_dw_w.astype(f32), row(conv_dw_b), row(conv_ln_g), row(conv_ln_b),
      w_conv_out.astype(bf), row(b_conv_out))

    n_hp = N_HEADS // HEADS_PER_STEP
    o = pl.pallas_call(
        _attn_kernel,
        grid=(B, n_hp, S // TQ),
        in_specs=[pl.BlockSpec((None, HEADS_PER_STEP * K_SLOT, TQ), lambda b, h, i: (b, h, i)),
                  pl.BlockSpec((None, S, HEADS_PER_STEP * K_SLOT), lambda b, h, i: (b, 0, h)),
                  pl.BlockSpec((None, HEADS_PER_STEP * V_HEAD_DIM, S), lambda b, h, i: (b, h, 0))],
        out_specs=pl.BlockSpec((None, TQ, HEADS_PER_STEP * V_HEAD_DIM), lambda b, h, i: (b, i, h)),
        out_shape=jax.ShapeDtypeStruct((B, S, MLA_WIDTH), bf),
        scratch_shapes=[
            pltpu.VMEM((2, HEADS_PER_STEP * (TQ // TQ_SUB), TK, TQ_SUB), f32),
            pltpu.VMEM((HEADS_PER_STEP * (TQ // TQ_SUB), V_HEAD_DIM + 16, TQ_SUB), f32)],
        compiler_params=params(dimension_semantics=("arbitrary", "arbitrary", "arbitrary")),
        name="attn",
    )(q_t, k, v_t)

    x1, h2 = pl.pallas_call(
        _merge_kernel,
        grid=(B, n_t),
        in_specs=[tok(MLA_WIDTH), tok(D),
                  pl.BlockSpec((None, TM, D), lambda b, i: (b, i, 1)),
                  tok(D), _resident((MLA_WIDTH, D)), _resident((D, D)), _resident((1, D))],
        out_specs=[tok(D), tok(D)],
        out_shape=[jax.ShapeDtypeStruct((B, S, D), f32), jax.ShapeDtypeStruct((B, S, D), bf)],
        compiler_params=params(dimension_semantics=("arbitrary", "arbitrary")),
        name="merge",
    )(o, c, g, x, w_mla_out.astype(bf), w_out.astype(bf), row(norm2_g))

    def by_chunk(w):
        r = w.shape[0]
        return (w.reshape(r, 2, D_FF // FFN_CHUNK, FFN_CHUNK).transpose(0, 2, 1, 3)
                .reshape(r, 2 * D_FF))

    fb = TM // FFN_HALO
    n_fb = S // FFN_HALO
    out = pl.pallas_call(
        _ffn_kernel,
        grid=(B, n_t),
        in_specs=[tok(D),
                  pl.BlockSpec((None, FFN_HALO, D), lambda b, i: (b, jnp.maximum(i * fb - 1, 0), 0)),
                  pl.BlockSpec((None, FFN_HALO, D),
                               lambda b, i: (b, jnp.minimum((i + 1) * fb, n_fb - 1), 0)),
                  tok(D), _resident((D, 2 * D_FF)), _resident((FFN_KERNEL, 2 * D_FF)),
                  _resident((1, 2 * D_FF)), _resident((D_FF, D)), _resident((1, D))],
        out_specs=tok(D),
        out_shape=jax.ShapeDtypeStruct((B, S, D), f32),
        scratch_shapes=[pltpu.VMEM((TM + 2 * FFN_HALO, D), bf),
                        pltpu.VMEM((2, TM + 2 * FFN_HALO, 2 * FFN_CHUNK), f32),
                        pltpu.VMEM((2, TM, FFN_CHUNK), bf),
                        pltpu.VMEM((TM, D), f32)],
        compiler_params=params(dimension_semantics=("arbitrary", "arbitrary")),
        name="ffn",
    )(h2, h2, h2, x1, by_chunk(w_ffn_up).astype(bf), by_chunk(ffn_dw_w).astype(f32),
      by_chunk(row(ffn_dw_b)), w_ffn_down.astype(bf), row(norm_f_g))
    return out


def kernel(x, positions, norm1_g, w_in, conv_dw_w, conv_dw_b, conv_ln_g, conv_ln_b, w_conv_out,
           b_conv_out, q_norm_g, w_uq, kv_norm_g, w_ukv, w_mla_out, w_out, norm2_g, w_ffn_up,
           ffn_dw_w, ffn_dw_b, w_ffn_down, norm_f_g):
    assert norm1_g.shape[0] == 1, "single-layer block"
    return _layer(x, positions, norm1_g[0], w_in[0], conv_dw_w[0], conv_dw_b[0], conv_ln_g[0],
                  conv_ln_b[0], w_conv_out[0], b_conv_out[0], q_norm_g[0], w_uq[0], kv_norm_g[0],
                  w_ukv[0], w_mla_out[0], w_out[0], norm2_g[0], w_ffn_up[0], ffn_dw_w[0],
                  ffn_dw_b[0], w_ffn_down[0], norm_f_g)
```

```python
import functools
import math

import jax
import jax.numpy as jnp
from jax import lax
from jax.experimental import pallas as pl
from jax.experimental.pallas import tpu as pltpu

D_MODEL = 1024
CONV_WIDTH = 512
CONV_KERNEL = 31
N_HEADS = 8
QK_NOPE_DIM = 64
QK_ROPE_DIM = 32
V_HEAD_DIM = 64
Q_LORA_RANK = 384
KV_LORA_RANK = 256
ROPE_THETA = 10000.0
QK_HEAD_DIM = QK_NOPE_DIM + QK_ROPE_DIM
MLA_WIDTH = N_HEADS * V_HEAD_DIM
D_FF = 2816
FFN_KERNEL = 3
NORM_EPS = 1e-6

LANES = 128
HALF_ROPE = QK_ROPE_DIM // 2
K_SLOT = LANES
HEADS_PER_STEP = 2
VMEM_LIMIT = 56 * 1024 * 1024

TM = 512
CONV_HALO = 16
CONV_ROWS = 128
FFN_HALO = 16
FFN_CHUNK = 256
TQ = 512
TQ_SUB = 256
TK = 512
ATTN_UNROLL = 4

_NT = (((1,), (1,)), ((), ()))


def _rms(x, g):
    return x * lax.rsqrt(jnp.mean(x * x, axis=-1, keepdims=True) + NORM_EPS) * g


def _dot(a, b):
    return jnp.dot(a, b, preferred_element_type=jnp.float32)


def _dot_nt(a, b):
    return lax.dot_general(a, b, _NT, preferred_element_type=jnp.float32)


def _resident(shape):
    nd = len(shape)
    return pl.BlockSpec(shape, lambda *_: (0,) * nd, pipeline_mode=pl.Buffered(1))


def _proj_kernel(x_ref, pos_ref, g1_ref, wa_ref, wq_ref, wkv_ref, wkrT_ref, wg_ref,
                 qng_ref, kvng_ref, wuqT_ref, wukp_ref, wuvT_ref, invf_ref,
                 u_ref, g_ref, qT_ref, k_ref, vT_ref, *, q_scale):
    bf = jnp.bfloat16
    hb = _rms(x_ref[...], g1_ref[...]).astype(bf)

    a = _dot(hb, wa_ref[...])
    u_ref[...] = a[:, :CONV_WIDTH] * jax.nn.sigmoid(a[:, CONV_WIDTH:])
    g_ref[...] = jax.nn.sigmoid(_dot(hb, wg_ref[...])).astype(bf)

    qn = _rms(_dot(hb, wq_ref[...]), qng_ref[...]).astype(bf)
    kvn = _rms(_dot(hb, wkv_ref[...]), kvng_ref[...]).astype(bf)

    ang = invf_ref[...] * pos_ref[...].astype(jnp.float32)
    cos_t, sin_t = jnp.cos(ang), jnp.sin(ang)

    def rope_t(t1, t2):
        return t1 * cos_t - t2 * sin_t, t2 * cos_t + t1 * sin_t

    q_t = _dot_nt(wuqT_ref[...], qn)
    tm = q_t.shape[1]
    slot_pad = jnp.zeros((K_SLOT - QK_HEAD_DIM, tm), jnp.float32)
    for h in range(N_HEADS):
        r0 = h * QK_HEAD_DIM
        r1, r2 = rope_t(q_t[r0 + QK_NOPE_DIM:r0 + QK_NOPE_DIM + HALF_ROPE],
                        q_t[r0 + QK_NOPE_DIM + HALF_ROPE:r0 + QK_HEAD_DIM])
        blk = jnp.concatenate([q_t[r0:r0 + QK_NOPE_DIM] * q_scale, r1 * q_scale, r2 * q_scale, slot_pad],
                              axis=0)
        qT_ref[h * K_SLOT:(h + 1) * K_SLOT, :] = blk.astype(bf)

    kr_t = _dot_nt(wkrT_ref[...], hb)
    k1, k2 = rope_t(kr_t[:HALF_ROPE], kr_t[HALF_ROPE:])
    kr_slot_t = jnp.concatenate(
        [jnp.zeros((QK_NOPE_DIM, tm), jnp.float32), k1, k2,
         jnp.zeros((K_SLOT - QK_HEAD_DIM, tm), jnp.float32)], axis=0)
    kr_slot = kr_slot_t.T
    knp = _dot(kvn, wukp_ref[...])
    for h in range(N_HEADS):
        c0 = h * K_SLOT
        k_ref[:, c0:c0 + K_SLOT] = (knp[:, c0:c0 + K_SLOT] + kr_slot).astype(bf)

    vT_ref[...] = _dot_nt(wuvT_ref[...], kvn).astype(bf)


def _conv_kernel(u_ref, up_ref, un_ref, g_ref, dww_ref, dwb_ref, lng_ref, lnb_ref,
                 wo_ref, bo_ref, c_ref, ext_ref, act_ref):
    i = pl.program_id(1)
    n = pl.num_programs(1)
    tm = u_ref.shape[0]
    n_lt = CONV_WIDTH // LANES
    lane = lambda l: slice(l * LANES, (l + 1) * LANES)
    for l in range(n_lt):
        ext_ref[l, 0:CONV_HALO, :] = jnp.where(i > 0, up_ref[:, lane(l)], 0.0)
        ext_ref[l, CONV_HALO:CONV_HALO + tm, :] = u_ref[:, lane(l)]
        ext_ref[l, CONV_HALO + tm:, :] = jnp.where(i < n - 1, un_ref[:, lane(l)], 0.0)

    half = CONV_ROWS // 2
    first = CONV_HALO - CONV_KERNEL // 2

    def rows_chunk(ci, carry):
        r0 = ci * CONV_ROWS
        for par in range(2):
            ys = []
            for l in range(n_lt):
                acc = jnp.zeros((half, LANES), jnp.float32)
                for k in range(CONV_KERNEL):
                    tap = ext_ref[l, pl.ds(first + r0 + par + k, half, stride=2), :]
                    acc = acc + tap * dww_ref[k:k + 1, lane(l)]
                ys.append(acc)
            y = jnp.concatenate(ys, axis=1) + dwb_ref[...]
            mu = jnp.mean(y, axis=-1, keepdims=True)
            yc = y - mu
            yn = yc * lax.rsqrt(jnp.mean(yc * yc, axis=-1, keepdims=True) + NORM_EPS)
            yn = yn * lng_ref[...] + lnb_ref[...]
            a = yn * jax.nn.sigmoid(yn)
            for l in range(n_lt):
                act_ref[l, pl.ds(r0 + par, half, stride=2), :] = a[:, lane(l)]
        return carry

    lax.fori_loop(0, tm // CONV_ROWS, rows_chunk, 0)
    act = jnp.concatenate([act_ref[l] for l in range(n_lt)], axis=1).astype(jnp.bfloat16)
    y_conv = _dot(act, wo_ref[...]) + bo_ref[...]
    c_ref[...] = (g_ref[...].astype(jnp.float32) * y_conv).astype(jnp.bfloat16)


def _attn_kernel(qT_ref, k_ref, vT_ref, o_ref, s_buf, acc_ref):
    bf = jnp.bfloat16
    seq = k_ref.shape[0]
    tq = qT_ref.shape[1]
    n_chunks = seq // TK
    n_sub = tq // TQ_SUB
    chains = [(h, qs) for h in range(HEADS_PER_STEP) for qs in range(n_sub)]
    ones_rows = jnp.ones((16, TK), bf)

    def scores(j, slot):
        r0 = j * TK if isinstance(j, int) else pl.multiple_of(j * TK, TK)
        tops = []
        for c, (h, qs) in enumerate(chains):
            k_blk = k_ref[pl.ds(r0, TK), h * K_SLOT:(h + 1) * K_SLOT]
            q_t = qT_ref[h * K_SLOT:(h + 1) * K_SLOT, qs * TQ_SUB:(qs + 1) * TQ_SUB]
            s_t = _dot(k_blk, q_t)
            s_buf[slot, c] = s_t
            tops.append(jnp.max(s_t, axis=0, keepdims=True))
        return tuple(tops)

    def consume(j, slot, ms, tops):
        r0 = j * TK if isinstance(j, int) else pl.multiple_of(j * TK, TK)
        out = []
        for c, (h, qs) in enumerate(chains):
            m_new = jnp.maximum(ms[c], tops[c])
            p_t = jnp.exp2(s_buf[slot, c] - m_new).astype(bf)
            v_ext = jnp.concatenate(
                [vT_ref[h * V_HEAD_DIM:(h + 1) * V_HEAD_DIM, pl.ds(r0, TK)], ones_rows], axis=0)
            acc_ref[c] = jnp.exp2(ms[c] - m_new) * acc_ref[c] + _dot(v_ext, p_t)
            out.append(m_new)
        return tuple(out)

    acc_ref[...] = jnp.zeros_like(acc_ref)
    ms = tuple(jnp.full((1, TQ_SUB), -jnp.inf, jnp.float32) for _ in chains)
    tops = scores(0, 0)

    def group(first, carry, last=False):
        ms, tops = carry
        for u in range(ATTN_UNROLL):
            nxt = None if (last and u == ATTN_UNROLL - 1) else scores(first + u + 1, (u + 1) % 2)
            ms = consume(first + u, u % 2, ms, tops)
            tops = nxt
        return ms, tops

    n_groups = n_chunks // ATTN_UNROLL
    carry = lax.fori_loop(0, n_groups - 1, lambda i, cr: group(i * ATTN_UNROLL, cr), (ms, tops))
    group((n_groups - 1) * ATTN_UNROLL, carry, last=True)

    for qs in range(n_sub):
        o_t = jnp.concatenate(
            [acc_ref[h * n_sub + qs, :V_HEAD_DIM] / acc_ref[h * n_sub + qs, V_HEAD_DIM:V_HEAD_DIM + 1]
             for h in range(HEADS_PER_STEP)], axis=0)
        o_ref[qs * TQ_SUB:(qs + 1) * TQ_SUB, :] = o_t.T.astype(bf)


def _merge_kernel(o_ref, c_ref, g_ref, x_ref, wm_ref, wo_ref, n2_ref, x1_ref, h2_ref):
    y_mla = _dot(o_ref[...], wm_ref[...])
    merged = c_ref[...].astype(jnp.float32) + g_ref[...].astype(jnp.float32) * y_mla
    x1 = x_ref[...] + _dot(merged.astype(jnp.bfloat16), wo_ref[...])
    x1_ref[...] = x1
    h2_ref[...] = _rms(x1, n2_ref[...]).astype(jnp.bfloat16)


def _ffn_kernel(h_ref, hp_ref, hn_ref, x1_ref, wup_ref, dww_ref, dwb_ref, wdn_ref, nf_ref,
                out_ref, hext_ref, z_buf, act_buf, acc_ref):
    i = pl.program_id(1)
    n = pl.num_programs(1)
    tm = h_ref.shape[0]
    rows = tm + 2 * FFN_HALO
    n_c = D_FF // FFN_CHUNK
    main = slice(FFN_HALO, FFN_HALO + tm)
    hext_ref[0:FFN_HALO, :] = jnp.where(i > 0, hp_ref[...], jnp.zeros_like(hp_ref))
    hext_ref[main, :] = h_ref[...]
    hext_ref[FFN_HALO + tm:, :] = jnp.where(i < n - 1, hn_ref[...], jnp.zeros_like(hn_ref))

    cols = lambda c, part: slice(part * D_FF + c * FFN_CHUNK, part * D_FF + (c + 1) * FFN_CHUNK)

    def up_proj(c):
        for part in range(2):
            z_buf[c % 2, part] = _dot(hext_ref[...], wup_ref[:, cols(c, part)])

    def conv3(c, part):
        z = z_buf[c % 2, part]
        w = dww_ref[:, cols(c, part)]
        return (pltpu.roll(z, 1, 0)[main] * w[0:1] + z[main] * w[1:2]
                + pltpu.roll(z, rows - 1, 0)[main] * w[2:3] + dwb_ref[:, cols(c, part)])

    def activate(c):
        gate, up = conv3(c, 0), conv3(c, 1)
        act_buf[c % 2] = (gate * jax.nn.sigmoid(gate) * up).astype(jnp.bfloat16)

    def down_proj(c):
        acc_ref[...] += _dot(act_buf[c % 2], wdn_ref[c * FFN_CHUNK:(c + 1) * FFN_CHUNK, :])

    acc_ref[...] = x1_ref[...]
    up_proj(0)
    up_proj(1)
    activate(0)
    for c in range(n_c):
        if c + 2 < n_c:
            up_proj(c + 2)
        if c + 1 < n_c:
            activate(c + 1)
        down_proj(c)
    out_ref[...] = _rms(acc_ref[...], nf_ref[...])


def _layer(x, positions, norm1_g, w_in, conv_dw_w, conv_dw_b, conv_ln_g, conv_ln_b,
           w_conv_out, b_conv_out, q_norm_g, w_uq, kv_norm_g, w_ukv, w_mla_out, w_out,
           norm2_g, w_ffn_up, ffn_dw_w, ffn_dw_b, w_ffn_down, norm_f_g):
    bf = jnp.bfloat16
    f32 = jnp.float32
    B, S, D = x.shape
    assert D == D_MODEL and S % TM == 0 and S % TQ == 0 and S % TK == 0
    n_t = S // TM
    row = lambda v: v.reshape(1, -1).astype(f32)
    params = functools.partial(pltpu.CompilerParams, vmem_limit_bytes=VMEM_LIMIT)

    o_q = 2 * CONV_WIDTH
    o_kv = o_q + Q_LORA_RANK
    o_kr = o_kv + KV_LORA_RANK
    o_g = o_kr + QK_ROPE_DIM
    wa = w_in[:, :o_q].astype(bf)
    wq = w_in[:, o_q:o_kv].astype(bf)
    wkv = w_in[:, o_kv:o_kr].astype(bf)
    wkrT = w_in[:, o_kr:o_g].T.astype(bf)
    wg = w_in[:, o_g:].astype(bf)
    wuqT = w_uq.T.astype(bf)
    w_ukv_h = w_ukv.reshape(KV_LORA_RANK, N_HEADS, QK_NOPE_DIM + V_HEAD_DIM)
    wukp = jnp.pad(w_ukv_h[:, :, :QK_NOPE_DIM], ((0, 0), (0, 0), (0, K_SLOT - QK_NOPE_DIM)))
    wukp = wukp.reshape(KV_LORA_RANK, N_HEADS * K_SLOT).astype(bf)
    wuvT = w_ukv_h[:, :, QK_NOPE_DIM:].reshape(KV_LORA_RANK, MLA_WIDTH).T.astype(bf)
    inv_freq = 1.0 / (ROPE_THETA ** (jnp.arange(0, QK_ROPE_DIM, 2, dtype=f32) / QK_ROPE_DIM))
    q_scale = (QK_HEAD_DIM ** -0.5) * math.log2(math.e)

    tok = lambda w: pl.BlockSpec((None, TM, w), lambda b, i: (b, i, 0))
    tok_t = lambda r: pl.BlockSpec((None, r, TM), lambda b, i: (b, 0, i))

    u, g, q_t, k, v_t = pl.pallas_call(
        functools.partial(_proj_kernel, q_scale=q_scale),
        grid=(B, n_t),
        in_specs=[tok(D), pl.BlockSpec((None, 1, TM), lambda b, i: (b, 0, i)),
                  _resident((1, D)), _resident(wa.shape), _resident(wq.shape), _resident(wkv.shape),
                  _resident(wkrT.shape), _resident(wg.shape), _resident((1, Q_LORA_RANK)),
                  _resident((1, KV_LORA_RANK)), _resident(wuqT.shape), _resident(wukp.shape),
                  _resident(wuvT.shape), _resident((HALF_ROPE, 1))],
        out_specs=[tok(CONV_WIDTH), tok(2 * D), tok_t(N_HEADS * K_SLOT),
                   tok(N_HEADS * K_SLOT), tok_t(MLA_WIDTH)],
        out_shape=[jax.ShapeDtypeStruct((B, S, CONV_WIDTH), f32),
                   jax.ShapeDtypeStruct((B, S, 2 * D), bf),
                   jax.ShapeDtypeStruct((B, N_HEADS * K_SLOT, S), bf),
                   jax.ShapeDtypeStruct((B, S, N_HEADS * K_SLOT), bf),
                   jax.ShapeDtypeStruct((B, MLA_WIDTH, S), bf)],
        compiler_params=params(dimension_semantics=("arbitrary", "arbitrary")),
        name="proj",
    )(x, positions.reshape(B, 1, S), row(norm1_g), wa, wq, wkv, wkrT, wg, row(q_norm_g),
      row(kv_norm_g), wuqT, wukp, wuvT, inv_freq.reshape(HALF_ROPE, 1))

    hb = TM // CONV_HALO
    n_hb = S // CONV_HALO
    c = pl.pallas_call(
        _conv_kernel,
        grid=(B, n_t),
        in_specs=[tok(CONV_WIDTH),
                  pl.BlockSpec((None, CONV_HALO, CONV_WIDTH),
                               lambda b, i: (b, jnp.maximum(i * hb - 1, 0), 0)),
                  pl.BlockSpec((None, CONV_HALO, CONV_WIDTH),
                               lambda b, i: (b, jnp.minimum((i + 1) * hb, n_hb - 1), 0)),
                  tok(D),
                  _resident((CONV_KERNEL, CONV_WIDTH)), _resident((1, CONV_WIDTH)),
                  _resident((1, CONV_WIDTH)), _resident((1, CONV_WIDTH)),
                  _resident((CONV_WIDTH, D)), _resident((1, D))],
        out_specs=tok(D),
        out_shape=jax.ShapeDtypeStruct((B, S, D), bf),
        scratch_shapes=[pltpu.VMEM((CONV_WIDTH // LANES, TM + 2 * CONV_HALO, LANES), f32),
                        pltpu.VMEM((CONV_WIDTH // LANES, TM, LANES), f32)],
        compiler_params=params(dimension_semantics=("arbitrary", "arbitrary")),
        name="conv",
    )(u, u, u, g, conv_dw_w.astype(f32), row(conv_dw_b), row(conv_ln_g), row(conv_ln_b),
      w_conv_out.astype(bf), row(b_conv_out))

    n_hp = N_HEADS // HEADS_PER_STEP
    o = pl.pallas_call(
        _attn_kernel,
        grid=(B, n_hp, S // TQ),
        in_specs=[pl.BlockSpec((None, HEADS_PER_STEP * K_SLOT, TQ), lambda b, h, i: (b, h, i)),
                  pl.BlockSpec((None, S, HEADS_PER_STEP * K_SLOT), lambda b, h, i: (b, 0, h)),
                  pl.BlockSpec((None, HEADS_PER_STEP * V_HEAD_DIM, S), lambda b, h, i: (b, h, 0))],
        out_specs=pl.BlockSpec((None, TQ, HEADS_PER_STEP * V_HEAD_DIM), lambda b, h, i: (b, i, h)),
        out_shape=jax.ShapeDtypeStruct((B, S, MLA_WIDTH), bf),
        scratch_shapes=[
            pltpu.VMEM((2, HEADS_PER_STEP * (TQ // TQ_SUB), TK, TQ_SUB), f32),
            pltpu.VMEM((HEADS_PER_STEP * (TQ // TQ_SUB), V_HEAD_DIM + 16, TQ_SUB), f32)],
        compiler_params=params(dimension_semantics=("arbitrary", "arbitrary", "arbitrary")),
        name="attn",
    )(q_t, k, v_t)

    x1, h2 = pl.pallas_call(
        _merge_kernel,
        grid=(B, n_t),
        in_specs=[tok(MLA_WIDTH), tok(D),
                  pl.BlockSpec((None, TM, D), lambda b, i: (b, i, 1)),
                  tok(D), _resident((MLA_WIDTH, D)), _resident((D, D)), _resident((1, D))],
        out_specs=[tok(D), tok(D)],
        out_shape=[jax.ShapeDtypeStruct((B, S, D), f32), jax.ShapeDtypeStruct((B, S, D), bf)],
        compiler_params=params(dimension_semantics=("arbitrary", "arbitrary")),
        name="merge",
    )(o, c, g, x, w_mla_out.astype(bf), w_out.astype(bf), row(norm2_g))

    fb = TM // FFN_HALO
    n_fb = S // FFN_HALO
    out = pl.pallas_call(
        _ffn_kernel,
        grid=(B, n_t),
        in_specs=[tok(D),
                  pl.BlockSpec((None, FFN_HALO, D), lambda b, i: (b, jnp.maximum(i * fb - 1, 0), 0)),
                  pl.BlockSpec((None, FFN_HALO, D),
                               lambda b, i: (b, jnp.minimum((i + 1) * fb, n_fb - 1), 0)),
                  tok(D), _resident((D, 2 * D_FF)), _resident((FFN_KERNEL, 2 * D_FF)),
                  _resident((1, 2 * D_FF)), _resident((D_FF, D)), _resident((1, D))],
        out_specs=tok(D),
        out_shape=jax.ShapeDtypeStruct((B, S, D), f32),
        scratch_shapes=[pltpu.VMEM((TM + 2 * FFN_HALO, D), bf),
                        pltpu.VMEM((2, 2, TM + 2 * FFN_HALO, FFN_CHUNK), f32),
                        pltpu.VMEM((2, TM, FFN_CHUNK), bf),
                        pltpu.VMEM((TM, D), f32)],
        compiler_params=params(dimension_semantics=("arbitrary", "arbitrary")),
        name="ffn",
    )(h2, h2, h2, x1, w_ffn_up.astype(bf), ffn_dw_w.astype(f32), row(ffn_dw_b),
      w_ffn_down.astype(bf), row(norm_f_g))
    return out


def kernel(x, positions, norm1_g, w_in, conv_dw_w, conv_dw_b, conv_ln_g, conv_ln_b, w_conv_out,
           b_conv_out, q_norm_g, w_uq, kv_norm_g, w_ukv, w_mla_out, w_out, norm2_g, w_ffn_up,
           ffn_dw_w, ffn_dw_b, w_ffn_down, norm_f_g):
    assert norm1_g.shape[0] == 1, "single-layer block"
    return _layer(x, positions, norm1_g[0], w_in[0], conv_dw_w[0], conv_dw_b[0], conv_ln_g[0],
                  conv_ln_b[0], w_conv_out[0], b_conv_out[0], q_norm_g[0], w_uq[0], kv_norm_g[0],
                  w_ukv[0], w_mla_out[0], w_out[0], norm2_g[0], w_ffn_up[0], ffn_dw_w[0],
                  ffn_dw_b[0], w_ffn_down[0], norm_f_g)
```

```python
import functools
import math

import jax
import jax.numpy as jnp
from jax import lax
from jax.experimental import pallas as pl
from jax.experimental.pallas import tpu as pltpu

D_MODEL = 1024
CONV_WIDTH = 512
CONV_KERNEL = 31
N_HEADS = 8
QK_NOPE_DIM = 64
QK_ROPE_DIM = 32
V_HEAD_DIM = 64
Q_LORA_RANK = 384
KV_LORA_RANK = 256
ROPE_THETA = 10000.0
QK_HEAD_DIM = QK_NOPE_DIM + QK_ROPE_DIM
MLA_WIDTH = N_HEADS * V_HEAD_DIM
D_FF = 2816
FFN_KERNEL = 3
NORM_EPS = 1e-6

LANES = 128
HALF_ROPE = QK_ROPE_DIM // 2
K_SLOT = LANES
HEADS_PER_STEP = 2
VMEM_LIMIT = 56 * 1024 * 1024

TM = 512
CONV_HALO = 16
CONV_ROWS = 128
FFN_HALO = 16
FFN_CHUNK = 256
TQ = 512
TQ_SUB = 256
TK = 512
ATTN_UNROLL = 4
BOUNDED_UNROLL = 4
BOUND_SLACK = 1.01
MIN_TRUSTED_DENOM = 2.0 ** -60

_NT = (((1,), (1,)), ((), ()))


def _rms(x, g):
    return x * lax.rsqrt(jnp.mean(x * x, axis=-1, keepdims=True) + NORM_EPS) * g


def _dot(a, b):
    return jnp.dot(a, b, preferred_element_type=jnp.float32)


def _dot_nt(a, b):
    return lax.dot_general(a, b, _NT, preferred_element_type=jnp.float32)


def _resident(shape):
    nd = len(shape)
    return pl.BlockSpec(shape, lambda *_: (0,) * nd, pipeline_mode=pl.Buffered(1))


def _proj_kernel(x_ref, pos_ref, g1_ref, wa_ref, wq_ref, wkv_ref, wkrT_ref, wg_ref,
                 qng_ref, kvng_ref, wuqT_ref, wukp_ref, wuvT_ref, invf_ref,
                 u_ref, g_ref, qT_ref, k_ref, vT_ref, kstat_ref, *, q_scale):
    bf = jnp.bfloat16
    hb = _rms(x_ref[...], g1_ref[...]).astype(bf)

    a = _dot(hb, wa_ref[...])
    u_ref[...] = a[:, :CONV_WIDTH] * jax.nn.sigmoid(a[:, CONV_WIDTH:])
    g_ref[...] = jax.nn.sigmoid(_dot(hb, wg_ref[...])).astype(bf)

    qn = _rms(_dot(hb, wq_ref[...]), qng_ref[...]).astype(bf)
    kvn = _rms(_dot(hb, wkv_ref[...]), kvng_ref[...]).astype(bf)

    ang = invf_ref[...] * pos_ref[...].astype(jnp.float32)
    cos_t, sin_t = jnp.cos(ang), jnp.sin(ang)

    def rope_t(t1, t2):
        return t1 * cos_t - t2 * sin_t, t2 * cos_t + t1 * sin_t

    q_t = _dot_nt(wuqT_ref[...], qn)
    tm = q_t.shape[1]
    slot_pad = jnp.zeros((K_SLOT - QK_HEAD_DIM, tm), jnp.float32)
    for h in range(N_HEADS):
        r0 = h * QK_HEAD_DIM
        r1, r2 = rope_t(q_t[r0 + QK_NOPE_DIM:r0 + QK_NOPE_DIM + HALF_ROPE],
                        q_t[r0 + QK_NOPE_DIM + HALF_ROPE:r0 + QK_HEAD_DIM])
        blk = jnp.concatenate([q_t[r0:r0 + QK_NOPE_DIM] * q_scale, r1 * q_scale, r2 * q_scale, slot_pad],
                              axis=0)
        qT_ref[h * K_SLOT:(h + 1) * K_SLOT, :] = blk.astype(bf)

    kr_t = _dot_nt(wkrT_ref[...], hb)
    k1, k2 = rope_t(kr_t[:HALF_ROPE], kr_t[HALF_ROPE:])
    tail_rows = lax.broadcasted_iota(jnp.int32, (K_SLOT - QK_HEAD_DIM, tm), 0)
    kr_slot_t = jnp.concatenate(
        [jnp.zeros((QK_NOPE_DIM, tm), jnp.float32), k1, k2,
         jnp.where(tail_rows == 0, 1.0, 0.0)], axis=0)
    kr_slot = kr_slot_t.T
    knp = _dot(kvn, wukp_ref[...])
    head_rows = lax.broadcasted_iota(jnp.int32, (N_HEADS, LANES), 0)
    kstat = jnp.zeros((N_HEADS, LANES), jnp.float32)
    for h in range(N_HEADS):
        c0 = h * K_SLOT
        k_h = knp[:, c0:c0 + K_SLOT] + kr_slot
        k_ref[:, c0:c0 + K_SLOT] = k_h.astype(bf)
        norm2 = jnp.max(jnp.sum(k_h * k_h, axis=-1, keepdims=True), axis=0, keepdims=True)
        kstat = jnp.where(head_rows == h, norm2, kstat)
    kstat_ref[...] = kstat

    vT_ref[...] = _dot_nt(wuvT_ref[...], kvn).astype(bf)


def _conv_kernel(u_ref, up_ref, un_ref, g_ref, dww_ref, dwb_ref, lng_ref, lnb_ref,
                 wo_ref, bo_ref, c_ref, ext_ref, act_ref):
    i = pl.program_id(1)
    n = pl.num_programs(1)
    tm = u_ref.shape[0]
    n_lt = CONV_WIDTH // LANES
    lane = lambda l: slice(l * LANES, (l + 1) * LANES)
    for l in range(n_lt):
        ext_ref[l, 0:CONV_HALO, :] = jnp.where(i > 0, up_ref[:, lane(l)], 0.0)
        ext_ref[l, CONV_HALO:CONV_HALO + tm, :] = u_ref[:, lane(l)]
        ext_ref[l, CONV_HALO + tm:, :] = jnp.where(i < n - 1, un_ref[:, lane(l)], 0.0)

    half = CONV_ROWS // 2
    first = CONV_HALO - CONV_KERNEL // 2

    def rows_chunk(ci, carry):
        r0 = ci * CONV_ROWS
        for par in range(2):
            ys = []
            for l in range(n_lt):
                acc = jnp.zeros((half, LANES), jnp.float32)
                for k in range(CONV_KERNEL):
                    tap = ext_ref[l, pl.ds(first + r0 + par + k, half, stride=2), :]
                    acc = acc + tap * dww_ref[k:k + 1, lane(l)]
                ys.append(acc)
            y = jnp.concatenate(ys, axis=1) + dwb_ref[...]
            mu = jnp.mean(y, axis=-1, keepdims=True)
            yc = y - mu
            yn = yc * lax.rsqrt(jnp.mean(yc * yc, axis=-1, keepdims=True) + NORM_EPS)
            yn = yn * lng_ref[...] + lnb_ref[...]
            a = yn * jax.nn.sigmoid(yn)
            for l in range(n_lt):
                act_ref[l, pl.ds(r0 + par, half, stride=2), :] = a[:, lane(l)]
        return carry

    lax.fori_loop(0, tm // CONV_ROWS, rows_chunk, 0)
    act = jnp.concatenate([act_ref[l] for l in range(n_lt)], axis=1).astype(jnp.bfloat16)
    y_conv = _dot(act, wo_ref[...]) + bo_ref[...]
    c_ref[...] = (g_ref[...].astype(jnp.float32) * y_conv).astype(jnp.bfloat16)


def _attn_chains(tq):
    n_sub = tq // TQ_SUB
    return [(h, qs) for h in range(HEADS_PER_STEP) for qs in range(n_sub)]


def _value_rows(vT_ref, h, r0):
    return jnp.concatenate([vT_ref[h * V_HEAD_DIM:(h + 1) * V_HEAD_DIM, pl.ds(r0, TK)],
                            jnp.ones((16, TK), jnp.bfloat16)], axis=0)


def _attn_running_max(qT_ref, k_ref, vT_ref, s_buf, acc_ref):
    bf = jnp.bfloat16
    n_chunks = k_ref.shape[0] // TK
    chains = _attn_chains(qT_ref.shape[1])

    def scores(j, slot):
        r0 = j * TK if isinstance(j, int) else pl.multiple_of(j * TK, TK)
        tops = []
        for c, (h, qs) in enumerate(chains):
            k_blk = k_ref[pl.ds(r0, TK), h * K_SLOT:(h + 1) * K_SLOT]
            q_t = qT_ref[h * K_SLOT:(h + 1) * K_SLOT, qs * TQ_SUB:(qs + 1) * TQ_SUB]
            s_t = _dot(k_blk, q_t)
            s_buf[slot, c] = s_t
            tops.append(jnp.max(s_t, axis=0, keepdims=True))
        return tuple(tops)

    def consume(j, slot, ms, tops):
        r0 = j * TK if isinstance(j, int) else pl.multiple_of(j * TK, TK)
        out = []
        for c, (h, qs) in enumerate(chains):
            m_new = jnp.maximum(ms[c], tops[c])
            p_t = jnp.exp2(s_buf[slot, c] - m_new).astype(bf)
            acc_ref[c] = jnp.exp2(ms[c] - m_new) * acc_ref[c] + _dot(_value_rows(vT_ref, h, r0), p_t)
            out.append(m_new)
        return tuple(out)

    acc_ref[...] = jnp.zeros_like(acc_ref)
    ms = tuple(jnp.full((1, TQ_SUB), -jnp.inf, jnp.float32) for _ in chains)
    tops = scores(0, 0)

    def group(first, carry, last=False):
        ms, tops = carry
        for u in range(ATTN_UNROLL):
            nxt = None if (last and u == ATTN_UNROLL - 1) else scores(first + u + 1, (u + 1) % 2)
            ms = consume(first + u, u % 2, ms, tops)
            tops = nxt
        return ms, tops

    n_groups = n_chunks // ATTN_UNROLL
    carry = lax.fori_loop(0, n_groups - 1, lambda i, cr: group(i * ATTN_UNROLL, cr), (ms, tops))
    group((n_groups - 1) * ATTN_UNROLL, carry, last=True)


def _attn_bounded(q_aug, k_ref, vT_ref, p_buf, acc_ref):
    n_chunks = k_ref.shape[0] // TK
    chains = _attn_chains(q_aug.shape[1])
    row0 = lambda j: j * TK if isinstance(j, int) else pl.multiple_of(j * TK, TK)

    def produce(j, slot):
        for c, (h, qs) in enumerate(chains):
            k_blk = k_ref[pl.ds(row0(j), TK), h * K_SLOT:(h + 1) * K_SLOT]
            q_t = q_aug[h * K_SLOT:(h + 1) * K_SLOT, qs * TQ_SUB:(qs + 1) * TQ_SUB]
            p_buf[slot, c] = jnp.exp2(_dot(k_blk, q_t)).astype(jnp.bfloat16)

    def consume(j, slot):
        for c, (h, qs) in enumerate(chains):
            acc_ref[c] += _dot(_value_rows(vT_ref, h, row0(j)), p_buf[slot, c])

    def group(first, last=False):
        for u in range(BOUNDED_UNROLL):
            if not (last and u == BOUNDED_UNROLL - 1):
                produce(first + u + 1, (u + 1) % 2)
            consume(first + u, u % 2)

    acc_ref[...] = jnp.zeros_like(acc_ref)
    produce(0, 0)
    n_groups = n_chunks // BOUNDED_UNROLL

    def body(i, carry):
        group(i * BOUNDED_UNROLL)
        return carry

    lax.fori_loop(0, n_groups - 1, body, 0)
    group((n_groups - 1) * BOUNDED_UNROLL, last=True)


def _attn_kernel(qT_ref, k_ref, vT_ref, kstat_ref, o_ref, q_aug, p_buf, s_buf, acc_ref):
    f32 = jnp.float32
    tq = qT_ref.shape[1]
    n_sub = tq // TQ_SUB
    hp = pl.program_id(1)
    key_norm2 = jnp.max(kstat_ref[...], axis=0)
    head_rows = lax.broadcasted_iota(jnp.int32, key_norm2.shape, 0)
    q_rows = lax.broadcasted_iota(jnp.int32, (K_SLOT, tq), 0)
    for h in range(HEADS_PER_STEP):
        q = qT_ref[h * K_SLOT:(h + 1) * K_SLOT, :].astype(f32)
        k2 = jnp.max(jnp.where(head_rows == HEADS_PER_STEP * hp + h, key_norm2, 0.0),
                     axis=0, keepdims=True)
        k2 = jnp.concatenate([k2] * (tq // LANES), axis=1)
        bound = jnp.sqrt(jnp.sum(q * q, axis=0, keepdims=True) * k2) * BOUND_SLACK
        q_aug[h * K_SLOT:(h + 1) * K_SLOT, :] = jnp.where(
            q_rows == QK_HEAD_DIM, -bound, q).astype(jnp.bfloat16)

    _attn_bounded(q_aug, k_ref, vT_ref, p_buf, acc_ref)
    denom_min = jnp.min(acc_ref[:, V_HEAD_DIM:V_HEAD_DIM + 1, :])
    trusted = denom_min > MIN_TRUSTED_DENOM

    @pl.when(jnp.logical_not(trusted))
    def _():
        _attn_running_max(qT_ref, k_ref, vT_ref, s_buf, acc_ref)

    for qs in range(n_sub):
        o_t = jnp.concatenate(
            [acc_ref[h * n_sub + qs, :V_HEAD_DIM] / acc_ref[h * n_sub + qs, V_HEAD_DIM:V_HEAD_DIM + 1]
             for h in range(HEADS_PER_STEP)], axis=0)
        o_ref[qs * TQ_SUB:(qs + 1) * TQ_SUB, :] = o_t.T.astype(jnp.bfloat16)


def _merge_kernel(o_ref, c_ref, g_ref, x_ref, wm_ref, wo_ref, n2_ref, x1_ref, h2_ref):
    y_mla = _dot(o_ref[...], wm_ref[...])
    merged = c_ref[...].astype(jnp.float32) + g_ref[...].astype(jnp.float32) * y_mla
    x1 = x_ref[...] + _dot(merged.astype(jnp.bfloat16), wo_ref[...])
    x1_ref[...] = x1
    h2_ref[...] = _rms(x1, n2_ref[...]).astype(jnp.bfloat16)


def _ffn_kernel(h_ref, hp_ref, hn_ref, x1_ref, wup_ref, dww_ref, dwb_ref, wdn_ref, nf_ref,
                out_ref, hext_ref, z_buf, act_ref, y_ref):
    i = pl.program_id(1)
    n = pl.num_programs(1)
    tm = h_ref.shape[0]
    half = tm // 2
    n_c = D_FF // FFN_CHUNK
    n_lt = FFN_CHUNK // LANES
    lane = lambda l: slice(l * LANES, (l + 1) * LANES)
    hext_ref[0:FFN_HALO, :] = jnp.where(i > 0, hp_ref[...], jnp.zeros_like(hp_ref))
    hext_ref[FFN_HALO:FFN_HALO + tm, :] = h_ref[...]
    hext_ref[FFN_HALO + tm:, :] = jnp.where(i < n - 1, hn_ref[...], jnp.zeros_like(hn_ref))

    def up_proj(c):
        for part in range(2):
            c0 = part * D_FF + c * FFN_CHUNK
            z = _dot(hext_ref[...], wup_ref[:, c0:c0 + FFN_CHUNK])
            for l in range(n_lt):
                z_buf[c % 2, part, l] = z[:, lane(l)]

    def conv3(c, part, par):
        ys = []
        for l in range(n_lt):
            c0 = part * D_FF + c * FFN_CHUNK + l * LANES
            w = dww_ref[:, c0:c0 + LANES]
            taps = [z_buf[c % 2, part, l, pl.ds(FFN_HALO + par + k - 1, half, stride=2), :]
                    for k in range(FFN_KERNEL)]
            ys.append(taps[0] * w[0:1] + taps[1] * w[1:2] + taps[2] * w[2:3]
                      + dwb_ref[:, c0:c0 + LANES])
        return jnp.concatenate(ys, axis=1)

    def activate(c):
        for par in range(2):
            gate, up = conv3(c, 0, par), conv3(c, 1, par)
            act_ref[par * half:(par + 1) * half, c * FFN_CHUNK:(c + 1) * FFN_CHUNK] = (
                gate * jax.nn.sigmoid(gate) * up).astype(jnp.bfloat16)

    up_proj(0)
    for c in range(n_c):
        if c + 1 < n_c:
            up_proj(c + 1)
        activate(c)

    y = _dot(act_ref[...], wdn_ref[...])
    for par in range(2):
        for l in range(D_MODEL // LANES):
            y_ref[l, pl.ds(par, half, stride=2), :] = y[par * half:(par + 1) * half, lane(l)]
    y_tok = jnp.concatenate([y_ref[l] for l in range(D_MODEL // LANES)], axis=1)
    out_ref[...] = _rms(x1_ref[...] + y_tok, nf_ref[...])


def _layer(x, positions, norm1_g, w_in, conv_dw_w, conv_dw_b, conv_ln_g, conv_ln_b,
           w_conv_out, b_conv_out, q_norm_g, w_uq, kv_norm_g, w_ukv, w_mla_out, w_out,
           norm2_g, w_ffn_up, ffn_dw_w, ffn_dw_b, w_ffn_down, norm_f_g):
    bf = jnp.bfloat16
    f32 = jnp.float32
    B, S, D = x.shape
    assert D == D_MODEL and S % TM == 0 and S % TQ == 0 and S % TK == 0
    n_t = S // TM
    row = lambda v: v.reshape(1, -1).astype(f32)
    params = functools.partial(pltpu.CompilerParams, vmem_limit_bytes=VMEM_LIMIT)

    o_q = 2 * CONV_WIDTH
    o_kv = o_q + Q_LORA_RANK
    o_kr = o_kv + KV_LORA_RANK
    o_g = o_kr + QK_ROPE_DIM
    wa = w_in[:, :o_q].astype(bf)
    wq = w_in[:, o_q:o_kv].astype(bf)
    wkv = w_in[:, o_kv:o_kr].astype(bf)
    wkrT = w_in[:, o_kr:o_g].T.astype(bf)
    wg = w_in[:, o_g:].astype(bf)
    wuqT = w_uq.T.astype(bf)
    w_ukv_h = w_ukv.reshape(KV_LORA_RANK, N_HEADS, QK_NOPE_DIM + V_HEAD_DIM)
    wukp = jnp.pad(w_ukv_h[:, :, :QK_NOPE_DIM], ((0, 0), (0, 0), (0, K_SLOT - QK_NOPE_DIM)))
    wukp = wukp.reshape(KV_LORA_RANK, N_HEADS * K_SLOT).astype(bf)
    wuvT = w_ukv_h[:, :, QK_NOPE_DIM:].reshape(KV_LORA_RANK, MLA_WIDTH).T.astype(bf)
    inv_freq = 1.0 / (ROPE_THETA ** (jnp.arange(0, QK_ROPE_DIM, 2, dtype=f32) / QK_ROPE_DIM))
    q_scale = (QK_HEAD_DIM ** -0.5) * math.log2(math.e)

    tok = lambda w: pl.BlockSpec((None, TM, w), lambda b, i: (b, i, 0))
    tok_t = lambda r: pl.BlockSpec((None, r, TM), lambda b, i: (b, 0, i))

    u, g, q_t, k, v_t, kstat = pl.pallas_call(
        functools.partial(_proj_kernel, q_scale=q_scale),
        grid=(B, n_t),
        in_specs=[tok(D), pl.BlockSpec((None, 1, TM), lambda b, i: (b, 0, i)),
                  _resident((1, D)), _resident(wa.shape), _resident(wq.shape), _resident(wkv.shape),
                  _resident(wkrT.shape), _resident(wg.shape), _resident((1, Q_LORA_RANK)),
                  _resident((1, KV_LORA_RANK)), _resident(wuqT.shape), _resident(wukp.shape),
                  _resident(wuvT.shape), _resident((HALF_ROPE, 1))],
        out_specs=[tok(CONV_WIDTH), tok(2 * D), tok_t(N_HEADS * K_SLOT),
                   tok(N_HEADS * K_SLOT), tok_t(MLA_WIDTH),
                   pl.BlockSpec((None, None, N_HEADS, LANES), lambda b, i: (b, i, 0, 0))],
        out_shape=[jax.ShapeDtypeStruct((B, S, CONV_WIDTH), f32),
                   jax.ShapeDtypeStruct((B, S, 2 * D), bf),
                   jax.ShapeDtypeStruct((B, N_HEADS * K_SLOT, S), bf),
                   jax.ShapeDtypeStruct((B, S, N_HEADS * K_SLOT), bf),
                   jax.ShapeDtypeStruct((B, MLA_WIDTH, S), bf),
                   jax.ShapeDtypeStruct((B, n_t, N_HEADS, LANES), f32)],
        compiler_params=params(dimension_semantics=("arbitrary", "arbitrary")),
        name="proj",
    )(x, positions.reshape(B, 1, S), row(norm1_g), wa, wq, wkv, wkrT, wg, row(q_norm_g),
      row(kv_norm_g), wuqT, wukp, wuvT, inv_freq.reshape(HALF_ROPE, 1))

    hb = TM // CONV_HALO
    n_hb = S // CONV_HALO
    c = pl.pallas_call(
        _conv_kernel,
        grid=(B, n_t),
        in_specs=[tok(CONV_WIDTH),
                  pl.BlockSpec((None, CONV_HALO, CONV_WIDTH),
                               lambda b, i: (b, jnp.maximum(i * hb - 1, 0), 0)),
                  pl.BlockSpec((None, CONV_HALO, CONV_WIDTH),
                               lambda b, i: (b, jnp.minimum((i + 1) * hb, n_hb - 1), 0)),
                  tok(D),
                  _resident((CONV_KERNEL, CONV_WIDTH)), _resident((1, CONV_WIDTH)),
                  _resident((1, CONV_WIDTH)), _resident((1, CONV_WIDTH)),
                  _resident((CONV_WIDTH, D)), _resident((1, D))],
        out_specs=tok(D),
        out_shape=jax.ShapeDtypeStruct((B, S, D), bf),
        scratch_shapes=[pltpu.VMEM((CONV_WIDTH // LANES, TM + 2 * CONV_HALO, LANES), f32),
                        pltpu.VMEM((CONV_WIDTH // LANES, TM, LANES), f32)],
        compiler_params=params(dimension_semantics=("arbitrary", "arbitrary")),
        name="conv",
    )(u, u, u, g, conv_dw_w.astype(f32), row(conv_dw_b), row(conv_ln_g), row(conv_ln_b),
      w_conv_out.astype(bf), row(b_conv_out))

    n_hp = N_HEADS // HEADS_PER_STEP
    o = pl.pallas_call(
        _attn_kernel,
        grid=(B, n_hp, S // TQ),
        in_specs=[pl.BlockSpec((None, HEADS_PER_STEP * K_SLOT, TQ), lambda b, h, i: (b, h, i)),
                  pl.BlockSpec((None, S, HEADS_PER_STEP * K_SLOT), lambda b, h, i: (b, 0, h)),
                  pl.BlockSpec((None, HEADS_PER_STEP * V_HEAD_DIM, S), lambda b, h, i: (b, h, 0)),
                  pl.BlockSpec((None, n_t, N_HEADS, LANES), lambda b, h, i: (b, 0, 0, 0))],
        out_specs=pl.BlockSpec((None, TQ, HEADS_PER_STEP * V_HEAD_DIM), lambda b, h, i: (b, i, h)),
        out_shape=jax.ShapeDtypeStruct((B, S, MLA_WIDTH), bf),
        scratch_shapes=[
            pltpu.VMEM((HEADS_PER_STEP * K_SLOT, TQ), bf),
            pltpu.VMEM((2, HEADS_PER_STEP * (TQ // TQ_SUB), TK, TQ_SUB), bf),
            pltpu.VMEM((2, HEADS_PER_STEP * (TQ // TQ_SUB), TK, TQ_SUB), f32),
            pltpu.VMEM((HEADS_PER_STEP * (TQ // TQ_SUB), V_HEAD_DIM + 16, TQ_SUB), f32)],
        compiler_params=params(dimension_semantics=("arbitrary", "arbitrary", "arbitrary")),
        name="attn",
    )(q_t, k, v_t, kstat)

    x1, h2 = pl.pallas_call(
        _merge_kernel,
        grid=(B, n_t),
        in_specs=[tok(MLA_WIDTH), tok(D),
                  pl.BlockSpec((None, TM, D), lambda b, i: (b, i, 1)),
                  tok(D), _resident((MLA_WIDTH, D)), _resident((D, D)), _resident((1, D))],
        out_specs=[tok(D), tok(D)],
        out_shape=[jax.ShapeDtypeStruct((B, S, D), f32), jax.ShapeDtypeStruct((B, S, D), bf)],
        compiler_params=params(dimension_semantics=("arbitrary", "arbitrary")),
        name="merge",
    )(o, c, g, x, w_mla_out.astype(bf), w_out.astype(bf), row(norm2_g))

    fb = TM // FFN_HALO
    n_fb = S // FFN_HALO
    out = pl.pallas_call(
        _ffn_kernel,
        grid=(B, n_t),
        in_specs=[tok(D),
                  pl.BlockSpec((None, FFN_HALO, D), lambda b, i: (b, jnp.maximum(i * fb - 1, 0), 0)),
                  pl.BlockSpec((None, FFN_HALO, D),
                               lambda b, i: (b, jnp.minimum((i + 1) * fb, n_fb - 1), 0)),
                  tok(D), _resident((D, 2 * D_FF)), _resident((FFN_KERNEL, 2 * D_FF)),
                  _resident((1, 2 * D_FF)), _resident((D_FF, D)), _resident((1, D))],
        out_specs=tok(D),
        out_shape=jax.ShapeDtypeStruct((B, S, D), f32),
        scratch_shapes=[pltpu.VMEM((TM + 2 * FFN_HALO, D), bf),
                        pltpu.VMEM((2, 2, FFN_CHUNK // LANES, TM + 2 * FFN_HALO, LANES), f32),
                        pltpu.VMEM((TM, D_FF), bf),
                        pltpu.VMEM((D // LANES, TM, LANES), f32)],
        compiler_params=params(dimension_semantics=("arbitrary", "arbitrary")),
        name="ffn",
    )(h2, h2, h2, x1, w_ffn_up.astype(bf), ffn_dw_w.astype(f32), row(ffn_dw_b),
      w_ffn_down.astype(bf), row(norm_f_g))
    return out


def kernel(x, positions, norm1_g, w_in, conv_dw_w, conv_dw_b, conv_ln_g, conv_ln_b, w_conv_out,
           b_conv_out, q_norm_g, w_uq, kv_norm_g, w_ukv, w_mla_out, w_out, norm2_g, w_ffn_up,
           ffn_dw_w, ffn_dw_b, w_ffn_down, norm_f_g):
    assert norm1_g.shape[0] == 1, "single-layer block"
    return _layer(x, positions, norm1_g[0], w_in[0], conv_dw_w[0], conv_dw_b[0], conv_ln_g[0],
                  conv_ln_b[0], w_conv_out[0], b_conv_out[0], q_norm_g[0], w_uq[0], kv_norm_g[0],
                  w_ukv[0], w_mla_out[0], w_out[0], norm2_g[0], w_ffn_up[0], ffn_dw_w[0],
                  ffn_dw_b[0], w_ffn_down[0], norm_f_g)
```

```python
import functools
import math

import jax
import jax.numpy as jnp
from jax import lax
from jax.experimental import pallas as pl
from jax.experimental.pallas import tpu as pltpu

D_MODEL = 1024
CONV_WIDTH = 512
CONV_KERNEL = 31
N_HEADS = 8
QK_NOPE_DIM = 64
QK_ROPE_DIM = 32
V_HEAD_DIM = 64
Q_LORA_RANK = 384
KV_LORA_RANK = 256
ROPE_THETA = 10000.0
QK_HEAD_DIM = QK_NOPE_DIM + QK_ROPE_DIM
MLA_WIDTH = N_HEADS * V_HEAD_DIM
D_FF = 2816
FFN_KERNEL = 3
NORM_EPS = 1e-6

LANES = 128
HALF_ROPE = QK_ROPE_DIM // 2
K_SLOT = LANES
HEADS_PER_STEP = 2
VMEM_LIMIT = 56 * 1024 * 1024

TM = 512
CONV_HALO = 16
CONV_ROWS = 128
FFN_HALO = 16
FFN_CHUNK = 256
TQ = 512
TQ_SUB = 512
TK = 256
ATTN_UNROLL = 4
BOUNDED_UNROLL = 16
BOUND_SLACK = 1.01
MIN_TRUSTED_DENOM = 2.0 ** -60

_NT = (((1,), (1,)), ((), ()))


def _rms(x, g):
    return x * lax.rsqrt(jnp.mean(x * x, axis=-1, keepdims=True) + NORM_EPS) * g


def _dot(a, b):
    return jnp.dot(a, b, preferred_element_type=jnp.float32)


def _dot_nt(a, b):
    return lax.dot_general(a, b, _NT, preferred_element_type=jnp.float32)


def _resident(shape):
    nd = len(shape)
    return pl.BlockSpec(shape, lambda *_: (0,) * nd, pipeline_mode=pl.Buffered(1))


def _proj_kernel(x_ref, pos_ref, g1_ref, wa_ref, wq_ref, wkv_ref, wkrT_ref, wg_ref,
                 qng_ref, kvng_ref, wuqT_ref, wukp_ref, wuvT_ref, invf_ref,
                 u_ref, g_ref, qT_ref, k_ref, vT_ref, kstat_ref, *, q_scale):
    bf = jnp.bfloat16
    hb = _rms(x_ref[...], g1_ref[...]).astype(bf)

    qn = _rms(_dot(hb, wq_ref[...]), qng_ref[...]).astype(bf)
    kvn = _rms(_dot(hb, wkv_ref[...]), kvng_ref[...]).astype(bf)

    ang = invf_ref[...] * pos_ref[...].astype(jnp.float32)
    cos_t, sin_t = jnp.cos(ang), jnp.sin(ang)

    def rope_t(t1, t2):
        return t1 * cos_t - t2 * sin_t, t2 * cos_t + t1 * sin_t

    q_t = _dot_nt(wuqT_ref[...], qn)
    tm = q_t.shape[1]
    slot_pad = jnp.zeros((K_SLOT - QK_HEAD_DIM, tm), jnp.float32)
    for h in range(N_HEADS):
        r0 = h * QK_HEAD_DIM
        r1, r2 = rope_t(q_t[r0 + QK_NOPE_DIM:r0 + QK_NOPE_DIM + HALF_ROPE],
                        q_t[r0 + QK_NOPE_DIM + HALF_ROPE:r0 + QK_HEAD_DIM])
        blk = jnp.concatenate([q_t[r0:r0 + QK_NOPE_DIM] * q_scale, r1 * q_scale, r2 * q_scale, slot_pad],
                              axis=0)
        qT_ref[h * K_SLOT:(h + 1) * K_SLOT, :] = blk.astype(bf)

    kr_t = _dot_nt(wkrT_ref[...], hb)
    k1, k2 = rope_t(kr_t[:HALF_ROPE], kr_t[HALF_ROPE:])
    tail_rows = lax.broadcasted_iota(jnp.int32, (K_SLOT - QK_HEAD_DIM, tm), 0)
    kr_slot_t = jnp.concatenate(
        [jnp.zeros((QK_NOPE_DIM, tm), jnp.float32), k1, k2,
         jnp.where(tail_rows == 0, 1.0, 0.0)], axis=0)
    kr_slot = kr_slot_t.T
    knp = _dot(kvn, wukp_ref[...])
    head_rows = lax.broadcasted_iota(jnp.int32, (N_HEADS, LANES), 0)
    kstat = jnp.zeros((N_HEADS, LANES), jnp.float32)
    for h in range(N_HEADS):
        c0 = h * K_SLOT
        k_h = knp[:, c0:c0 + K_SLOT] + kr_slot
        k_ref[:, c0:c0 + K_SLOT] = k_h.astype(bf)
        norm2 = jnp.max(jnp.sum(k_h * k_h, axis=-1, keepdims=True), axis=0, keepdims=True)
        kstat = jnp.where(head_rows == h, norm2, kstat)
    kstat_ref[...] = kstat

    vT_ref[...] = _dot_nt(wuvT_ref[...], kvn).astype(bf)

    a = _dot(hb, wa_ref[...])
    u_ref[...] = a[:, :CONV_WIDTH] * jax.nn.sigmoid(a[:, CONV_WIDTH:])
    g_ref[...] = jax.nn.sigmoid(_dot(hb, wg_ref[...])).astype(bf)


def _conv_kernel(u_ref, up_ref, un_ref, g_ref, dww_ref, dwb_ref, lng_ref, lnb_ref,
                 wo_ref, bo_ref, c_ref, ext_ref, act_ref):
    i = pl.program_id(1)
    n = pl.num_programs(1)
    tm = u_ref.shape[0]
    n_lt = CONV_WIDTH // LANES
    lane = lambda l: slice(l * LANES, (l + 1) * LANES)
    for l in range(n_lt):
        ext_ref[l, 0:CONV_HALO, :] = jnp.where(i > 0, up_ref[:, lane(l)], 0.0)
        ext_ref[l, CONV_HALO:CONV_HALO + tm, :] = u_ref[:, lane(l)]
        ext_ref[l, CONV_HALO + tm:, :] = jnp.where(i < n - 1, un_ref[:, lane(l)], 0.0)

    half = CONV_ROWS // 2
    first = CONV_HALO - CONV_KERNEL // 2

    def rows_chunk(ci):
        r0 = ci * CONV_ROWS
        for par in range(2):
            ys = []
            for l in range(n_lt):
                acc = jnp.zeros((half, LANES), jnp.float32)
                for k in range(CONV_KERNEL):
                    tap = ext_ref[l, pl.ds(first + r0 + par + k, half, stride=2), :]
                    acc = acc + tap * dww_ref[k:k + 1, lane(l)]
                ys.append(acc)
            y = jnp.concatenate(ys, axis=1) + dwb_ref[...]
            mu = jnp.mean(y, axis=-1, keepdims=True)
            yc = y - mu
            yn = yc * lax.rsqrt(jnp.mean(yc * yc, axis=-1, keepdims=True) + NORM_EPS)
            yn = yn * lng_ref[...] + lnb_ref[...]
            a = yn * jax.nn.sigmoid(yn)
            for l in range(n_lt):
                act_ref[l, pl.ds(r0 + par, half, stride=2), :] = a[:, lane(l)]

    for ci in range(tm // CONV_ROWS):
        rows_chunk(ci)
    act = jnp.concatenate([act_ref[l] for l in range(n_lt)], axis=1).astype(jnp.bfloat16)
    y_conv = _dot(act, wo_ref[...]) + bo_ref[...]
    c_ref[...] = (g_ref[...].astype(jnp.float32) * y_conv).astype(jnp.bfloat16)


def _attn_chains(tq):
    n_sub = tq // TQ_SUB
    return [(h, qs) for h in range(HEADS_PER_STEP) for qs in range(n_sub)]


def _value_rows(vT_ref, h, r0):
    return jnp.concatenate([vT_ref[h * V_HEAD_DIM:(h + 1) * V_HEAD_DIM, pl.ds(r0, TK)],
                            jnp.ones((16, TK), jnp.bfloat16)], axis=0)


def _attn_running_max(qT_ref, k_ref, vT_ref, s_buf, acc_ref):
    bf = jnp.bfloat16
    n_chunks = k_ref.shape[0] // TK
    chains = _attn_chains(qT_ref.shape[1])

    def scores(j, slot):
        r0 = j * TK if isinstance(j, int) else pl.multiple_of(j * TK, TK)
        tops = []
        for c, (h, qs) in enumerate(chains):
            k_blk = k_ref[pl.ds(r0, TK), h * K_SLOT:(h + 1) * K_SLOT]
            q_t = qT_ref[h * K_SLOT:(h + 1) * K_SLOT, qs * TQ_SUB:(qs + 1) * TQ_SUB]
            s_t = _dot(k_blk, q_t)
            s_buf[slot, c] = s_t
            tops.append(jnp.max(s_t, axis=0, keepdims=True))
        return tuple(tops)

    def consume(j, slot, ms, tops):
        r0 = j * TK if isinstance(j, int) else pl.multiple_of(j * TK, TK)
        out = []
        for c, (h, qs) in enumerate(chains):
            m_new = jnp.maximum(ms[c], tops[c])
            p_t = jnp.exp2(s_buf[slot, c] - m_new).astype(bf)
            acc_ref[c] = jnp.exp2(ms[c] - m_new) * acc_ref[c] + _dot(_value_rows(vT_ref, h, r0), p_t)
            out.append(m_new)
        return tuple(out)

    acc_ref[...] = jnp.zeros_like(acc_ref)
    ms = tuple(jnp.full((1, TQ_SUB), -jnp.inf, jnp.float32) for _ in chains)
    tops = scores(0, 0)

    def group(first, carry, last=False):
        ms, tops = carry
        for u in range(ATTN_UNROLL):
            nxt = None if (last and u == ATTN_UNROLL - 1) else scores(first + u + 1, (u + 1) % 2)
            ms = consume(first + u, u % 2, ms, tops)
            tops = nxt
        return ms, tops

    n_groups = n_chunks // ATTN_UNROLL
    carry = lax.fori_loop(0, n_groups - 1, lambda i, cr: group(i * ATTN_UNROLL, cr), (ms, tops))
    group((n_groups - 1) * ATTN_UNROLL, carry, last=True)


def _attn_bounded(q_aug, k_ref, vT_ref, p_buf, acc_ref):
    n_chunks = k_ref.shape[0] // TK
    chains = _attn_chains(q_aug.shape[1])
    row0 = lambda j: j * TK if isinstance(j, int) else pl.multiple_of(j * TK, TK)

    def produce(j, slot, only=None):
        for c, (h, qs) in enumerate(chains):
            if only is not None and c != only:
                continue
            k_blk = k_ref[pl.ds(row0(j), TK), h * K_SLOT:(h + 1) * K_SLOT]
            q_t = q_aug[h * K_SLOT:(h + 1) * K_SLOT, qs * TQ_SUB:(qs + 1) * TQ_SUB]
            p_buf[slot, c] = jnp.exp2(_dot(k_blk, q_t)).astype(jnp.bfloat16)

    def consume(j, slot, only=None):
        for c, (h, qs) in enumerate(chains):
            if only is not None and c != only:
                continue
            acc_ref[c] += _dot(_value_rows(vT_ref, h, row0(j)), p_buf[slot, c])

    def group(first, last=False):
        for u in range(BOUNDED_UNROLL):
            for c in range(len(chains)):
                if not (last and u == BOUNDED_UNROLL - 1):
                    produce(first + u + 1, (u + 1) % 2, only=c)
                consume(first + u, u % 2, only=c)

    acc_ref[...] = jnp.zeros_like(acc_ref)
    produce(0, 0)
    n_groups = n_chunks // BOUNDED_UNROLL

    def body(i, carry):
        group(i * BOUNDED_UNROLL)
        return carry

    lax.fori_loop(0, n_groups - 1, body, 0)
    group((n_groups - 1) * BOUNDED_UNROLL, last=True)


def _attn_kernel(qT_ref, k_ref, vT_ref, kstat_ref, o_ref, q_aug, p_buf, s_buf, acc_ref):
    f32 = jnp.float32
    tq = qT_ref.shape[1]
    n_sub = tq // TQ_SUB
    hp = pl.program_id(1)
    key_norm2 = jnp.max(kstat_ref[...], axis=0)
    head_rows = lax.broadcasted_iota(jnp.int32, key_norm2.shape, 0)
    q_rows = lax.broadcasted_iota(jnp.int32, (K_SLOT, tq), 0)
    for h in range(HEADS_PER_STEP):
        q = qT_ref[h * K_SLOT:(h + 1) * K_SLOT, :].astype(f32)
        k2 = jnp.max(jnp.where(head_rows == HEADS_PER_STEP * hp + h, key_norm2, 0.0),
                     axis=0, keepdims=True)
        k2 = jnp.concatenate([k2] * (tq // LANES), axis=1)
        bound = jnp.sqrt(jnp.sum(q * q, axis=0, keepdims=True) * k2) * BOUND_SLACK
        q_aug[h * K_SLOT:(h + 1) * K_SLOT, :] = jnp.where(
            q_rows == QK_HEAD_DIM, -bound, q).astype(jnp.bfloat16)

    _attn_bounded(q_aug, k_ref, vT_ref, p_buf, acc_ref)
    denom_min = jnp.min(acc_ref[:, V_HEAD_DIM:V_HEAD_DIM + 1, :])
    trusted = denom_min > MIN_TRUSTED_DENOM

    @pl.when(jnp.logical_not(trusted))
    def _():
        _attn_running_max(qT_ref, k_ref, vT_ref, s_buf, acc_ref)

    for qs in range(n_sub):
        o_t = jnp.concatenate(
            [acc_ref[h * n_sub + qs, :V_HEAD_DIM] / acc_ref[h * n_sub + qs, V_HEAD_DIM:V_HEAD_DIM + 1]
             for h in range(HEADS_PER_STEP)], axis=0)
        o_ref[qs * TQ_SUB:(qs + 1) * TQ_SUB, :] = o_t.T.astype(jnp.bfloat16)


def _merge_kernel(o_ref, c_ref, g_ref, x_ref, wm_ref, wo_ref, n2_ref, x1_ref, h2_ref):
    y_mla = _dot(o_ref[...], wm_ref[...])
    merged = c_ref[...].astype(jnp.float32) + g_ref[...].astype(jnp.float32) * y_mla
    x1 = x_ref[...] + _dot(merged.astype(jnp.bfloat16), wo_ref[...])
    x1_ref[...] = x1
    h2_ref[...] = _rms(x1, n2_ref[...]).astype(jnp.bfloat16)


def _ffn_kernel(h_ref, hp_ref, hn_ref, x1_ref, wup_ref, dww_ref, dwb_ref, wdn_ref, nf_ref,
                out_ref, hext_ref, z_buf, act_ref, y_ref):
    i = pl.program_id(1)
    n = pl.num_programs(1)
    tm = h_ref.shape[0]
    half = tm // 2
    n_c = D_FF // FFN_CHUNK
    n_lt = FFN_CHUNK // LANES
    lane = lambda l: slice(l * LANES, (l + 1) * LANES)
    hext_ref[0:FFN_HALO, :] = jnp.where(i > 0, hp_ref[...], jnp.zeros_like(hp_ref))
    hext_ref[FFN_HALO:FFN_HALO + tm, :] = h_ref[...]
    hext_ref[FFN_HALO + tm:, :] = jnp.where(i < n - 1, hn_ref[...], jnp.zeros_like(hn_ref))

    def up_proj(c):
        for part in range(2):
            c0 = part * D_FF + c * FFN_CHUNK
            z = _dot(hext_ref[...], wup_ref[:, c0:c0 + FFN_CHUNK])
            for l in range(n_lt):
                z_buf[c % 2, part, l] = z[:, lane(l)]

    def conv3(c, part, par):
        ys = []
        for l in range(n_lt):
            c0 = part * D_FF + c * FFN_CHUNK + l * LANES
            w = dww_ref[:, c0:c0 + LANES]
            taps = [z_buf[c % 2, part, l, pl.ds(FFN_HALO + par + k - 1, half, stride=2), :]
                    for k in range(FFN_KERNEL)]
            ys.append(taps[0] * w[0:1] + taps[1] * w[1:2] + taps[2] * w[2:3]
                      + dwb_ref[:, c0:c0 + LANES])
        return jnp.concatenate(ys, axis=1)

    def activate(c):
        for par in range(2):
            gate, up = conv3(c, 0, par), conv3(c, 1, par)
            act_ref[par * half:(par + 1) * half, c * FFN_CHUNK:(c + 1) * FFN_CHUNK] = (
                gate * jax.nn.sigmoid(gate) * up).astype(jnp.bfloat16)

    def down_proj(c):
        return _dot(act_ref[:, c * FFN_CHUNK:(c + 1) * FFN_CHUNK],
                    wdn_ref[c * FFN_CHUNK:(c + 1) * FFN_CHUNK, :])

    up_proj(0)
    y = None
    for c in range(n_c):
        if c + 1 < n_c:
            up_proj(c + 1)
        if c >= 1:
            y = down_proj(c - 1) if y is None else y + down_proj(c - 1)
        activate(c)
    y = y + down_proj(n_c - 1)
    for par in range(2):
        for l in range(D_MODEL // LANES):
            y_ref[l, pl.ds(par, half, stride=2), :] = y[par * half:(par + 1) * half, lane(l)]
    y_tok = jnp.concatenate([y_ref[l] for l in range(D_MODEL // LANES)], axis=1)
    out_ref[...] = _rms(x1_ref[...] + y_tok, nf_ref[...])


def _layer(x, positions, norm1_g, w_in, conv_dw_w, conv_dw_b, conv_ln_g, conv_ln_b,
           w_conv_out, b_conv_out, q_norm_g, w_uq, kv_norm_g, w_ukv, w_mla_out, w_out,
           norm2_g, w_ffn_up, ffn_dw_w, ffn_dw_b, w_ffn_down, norm_f_g):
    bf = jnp.bfloat16
    f32 = jnp.float32
    B, S, D = x.shape
    assert D == D_MODEL and S % TM == 0 and S % TQ == 0 and S % TK == 0
    n_t = S // TM
    row = lambda v: v.reshape(1, -1).astype(f32)
    params = functools.partial(pltpu.CompilerParams, vmem_limit_bytes=VMEM_LIMIT)

    o_q = 2 * CONV_WIDTH
    o_kv = o_q + Q_LORA_RANK
    o_kr = o_kv + KV_LORA_RANK
    o_g = o_kr + QK_ROPE_DIM
    wa = w_in[:, :o_q].astype(bf)
    wq = w_in[:, o_q:o_kv].astype(bf)
    wkv = w_in[:, o_kv:o_kr].astype(bf)
    wkrT = w_in[:, o_kr:o_g].T.astype(bf)
    wg = w_in[:, o_g:].astype(bf)
    wuqT = w_uq.T.astype(bf)
    w_ukv_h = w_ukv.reshape(KV_LORA_RANK, N_HEADS, QK_NOPE_DIM + V_HEAD_DIM)
    wukp = jnp.pad(w_ukv_h[:, :, :QK_NOPE_DIM], ((0, 0), (0, 0), (0, K_SLOT - QK_NOPE_DIM)))
    wukp = wukp.reshape(KV_LORA_RANK, N_HEADS * K_SLOT).astype(bf)
    wuvT = w_ukv_h[:, :, QK_NOPE_DIM:].reshape(KV_LORA_RANK, MLA_WIDTH).T.astype(bf)
    inv_freq = 1.0 / (ROPE_THETA ** (jnp.arange(0, QK_ROPE_DIM, 2, dtype=f32) / QK_ROPE_DIM))
    q_scale = (QK_HEAD_DIM ** -0.5) * math.log2(math.e)

    tok = lambda w: pl.BlockSpec((None, TM, w), lambda b, i: (b, i, 0))
    tok_t = lambda r: pl.BlockSpec((None, r, TM), lambda b, i: (b, 0, i))

    u, g, q_t, k, v_t, kstat = pl.pallas_call(
        functools.partial(_proj_kernel, q_scale=q_scale),
        grid=(B, n_t),
        in_specs=[tok(D), pl.BlockSpec((None, 1, TM), lambda b, i: (b, 0, i)),
                  _resident((1, D)), _resident(wa.shape), _resident(wq.shape), _resident(wkv.shape),
                  _resident(wkrT.shape), _resident(wg.shape), _resident((1, Q_LORA_RANK)),
                  _resident((1, KV_LORA_RANK)), _resident(wuqT.shape), _resident(wukp.shape),
                  _resident(wuvT.shape), _resident((HALF_ROPE, 1))],
        out_specs=[tok(CONV_WIDTH), tok(2 * D), tok_t(N_HEADS * K_SLOT),
                   tok(N_HEADS * K_SLOT), tok_t(MLA_WIDTH),
                   pl.BlockSpec((None, None, N_HEADS, LANES), lambda b, i: (b, i, 0, 0))],
        out_shape=[jax.ShapeDtypeStruct((B, S, CONV_WIDTH), f32),
                   jax.ShapeDtypeStruct((B, S, 2 * D), bf),
                   jax.ShapeDtypeStruct((B, N_HEADS * K_SLOT, S), bf),
                   jax.ShapeDtypeStruct((B, S, N_HEADS * K_SLOT), bf),
                   jax.ShapeDtypeStruct((B, MLA_WIDTH, S), bf),
                   jax.ShapeDtypeStruct((B, n_t, N_HEADS, LANES), f32)],
        compiler_params=params(dimension_semantics=("arbitrary", "arbitrary")),
        name="proj",
    )(x, positions.reshape(B, 1, S), row(norm1_g), wa, wq, wkv, wkrT, wg, row(q_norm_g),
      row(kv_norm_g), wuqT, wukp, wuvT, inv_freq.reshape(HALF_ROPE, 1))

    hb = TM // CONV_HALO
    n_hb = S // CONV_HALO
    c = pl.pallas_call(
        _conv_kernel,
        grid=(B, n_t),
        in_specs=[tok(CONV_WIDTH),
                  pl.BlockSpec((None, CONV_HALO, CONV_WIDTH),
                               lambda b, i: (b, jnp.maximum(i * hb - 1, 0), 0)),
                  pl.BlockSpec((None, CONV_HALO, CONV_WIDTH),
                               lambda b, i: (b, jnp.minimum((i + 1) * hb, n_hb - 1), 0)),
                  tok(D),
                  _resident((CONV_KERNEL, CONV_WIDTH)), _resident((1, CONV_WIDTH)),
                  _resident((1, CONV_WIDTH)), _resident((1, CONV_WIDTH)),
                  _resident((CONV_WIDTH, D)), _resident((1, D))],
        out_specs=tok(D),
        out_shape=jax.ShapeDtypeStruct((B, S, D), bf),
        scratch_shapes=[pltpu.VMEM((CONV_WIDTH // LANES, TM + 2 * CONV_HALO, LANES), f32),
                        pltpu.VMEM((CONV_WIDTH // LANES, TM, LANES), f32)],
        compiler_params=params(dimension_semantics=("arbitrary", "arbitrary")),
        name="conv",
    )(u, u, u, g, conv_dw_w.astype(f32), row(conv_dw_b), row(conv_ln_g), row(conv_ln_b),
      w_conv_out.astype(bf), row(b_conv_out))

    n_hp = N_HEADS // HEADS_PER_STEP
    o = pl.pallas_call(
        _attn_kernel,
        grid=(B, n_hp, S // TQ),
        in_specs=[pl.BlockSpec((None, HEADS_PER_STEP * K_SLOT, TQ), lambda b, h, i: (b, h, i)),
                  pl.BlockSpec((None, S, HEADS_PER_STEP * K_SLOT), lambda b, h, i: (b, 0, h)),
                  pl.BlockSpec((None, HEADS_PER_STEP * V_HEAD_DIM, S), lambda b, h, i: (b, h, 0)),
                  pl.BlockSpec((None, n_t, N_HEADS, LANES), lambda b, h, i: (b, 0, 0, 0))],
        out_specs=pl.BlockSpec((None, TQ, HEADS_PER_STEP * V_HEAD_DIM), lambda b, h, i: (b, i, h)),
        out_shape=jax.ShapeDtypeStruct((B, S, MLA_WIDTH), bf),
        scratch_shapes=[
            pltpu.VMEM((HEADS_PER_STEP * K_SLOT, TQ), bf),
            pltpu.VMEM((2, HEADS_PER_STEP * (TQ // TQ_SUB), TK, TQ_SUB), bf),
            pltpu.VMEM((2, HEADS_PER_STEP * (TQ // TQ_SUB), TK, TQ_SUB), f32),
            pltpu.VMEM((HEADS_PER_STEP * (TQ // TQ_SUB), V_HEAD_DIM + 16, TQ_SUB), f32)],
        compiler_params=params(dimension_semantics=("arbitrary", "arbitrary", "arbitrary")),
        name="attn",
    )(q_t, k, v_t, kstat)

    x1, h2 = pl.pallas_call(
        _merge_kernel,
        grid=(B, n_t),
        in_specs=[tok(MLA_WIDTH), tok(D),
                  pl.BlockSpec((None, TM, D), lambda b, i: (b, i, 1)),
                  tok(D), _resident((MLA_WIDTH, D)), _resident((D, D)), _resident((1, D))],
        out_specs=[tok(D), tok(D)],
        out_shape=[jax.ShapeDtypeStruct((B, S, D), f32), jax.ShapeDtypeStruct((B, S, D), bf)],
        compiler_params=params(dimension_semantics=("arbitrary", "arbitrary")),
        name="merge",
    )(o, c, g, x, w_mla_out.astype(bf), w_out.astype(bf), row(norm2_g))

    fb = TM // FFN_HALO
    n_fb = S // FFN_HALO
    out = pl.pallas_call(
        _ffn_kernel,
        grid=(B, n_t),
        in_specs=[tok(D),
                  pl.BlockSpec((None, FFN_HALO, D), lambda b, i: (b, jnp.maximum(i * fb - 1, 0), 0)),
                  pl.BlockSpec((None, FFN_HALO, D),
                               lambda b, i: (b, jnp.minimum((i + 1) * fb, n_fb - 1), 0)),
                  tok(D), _resident((D, 2 * D_FF)), _resident((FFN_KERNEL, 2 * D_FF)),
                  _resident((1, 2 * D_FF)), _resident((D_FF, D)), _resident((1, D))],
        out_specs=tok(D),
        out_shape=jax.ShapeDtypeStruct((B, S, D), f32),
        scratch_shapes=[pltpu.VMEM((TM + 2 * FFN_HALO, D), bf),
                        pltpu.VMEM((2, 2, FFN_CHUNK // LANES, TM + 2 * FFN_HALO, LANES), f32),
                        pltpu.VMEM((TM, D_FF), bf),
                        pltpu.VMEM((D // LANES, TM, LANES), f32)],
        compiler_params=params(dimension_semantics=("arbitrary", "arbitrary")),
        name="ffn",
    )(h2, h2, h2, x1, w_ffn_up.astype(bf), ffn_dw_w.astype(f32), row(ffn_dw_b),
      w_ffn_down.astype(bf), row(norm_f_g))
    return out


def kernel(x, positions, norm1_g, w_in, conv_dw_w, conv_dw_b, conv_ln_g, conv_ln_b, w_conv_out,
           b_conv_out, q_norm_g, w_uq, kv_norm_g, w_ukv, w_mla_out, w_out, norm2_g, w_ffn_up,
           ffn_dw_w, ffn_dw_b, w_ffn_down, norm_f_g):
    assert norm1_g.shape[0] == 1, "single-layer block"
    return _layer(x, positions, norm1_g[0], w_in[0], conv_dw_w[0], conv_dw_b[0], conv_ln_g[0],
                  conv_ln_b[0], w_conv_out[0], b_conv_out[0], q_norm_g[0], w_uq[0], kv_norm_g[0],
                  w_ukv[0], w_mla_out[0], w_out[0], norm2_g[0], w_ffn_up[0], ffn_dw_w[0],
                  ffn_dw_b[0], w_ffn_down[0], norm_f_g)
```

```python
import functools
import math

import jax
import jax.numpy as jnp
from jax import lax
from jax.experimental import pallas as pl
from jax.experimental.pallas import tpu as pltpu

D_MODEL = 1024
CONV_WIDTH = 512
CONV_KERNEL = 31
N_HEADS = 8
QK_NOPE_DIM = 64
QK_ROPE_DIM = 32
V_HEAD_DIM = 64
Q_LORA_RANK = 384
KV_LORA_RANK = 256
ROPE_THETA = 10000.0
QK_HEAD_DIM = QK_NOPE_DIM + QK_ROPE_DIM
MLA_WIDTH = N_HEADS * V_HEAD_DIM
D_FF = 2816
FFN_KERNEL = 3
NORM_EPS = 1e-6

LANES = 128
HALF_ROPE = QK_ROPE_DIM // 2
K_SLOT = LANES
HEADS_PER_STEP = 2
VMEM_LIMIT = 56 * 1024 * 1024

TM = 512
MERGE_TM = 1024
CONV_HALO = 16
CONV_ROWS = 128
FFN_HALO = 16
FFN_CHUNK = 256
TQ = 1024
TQ_SUB = 512
TK = 256
ATTN_UNROLL = 4
BOUNDED_UNROLL = 16
BOUND_SLACK = 1.01
MIN_TRUSTED_DENOM = 2.0 ** -60

_NT = (((1,), (1,)), ((), ()))


def _rms(x, g):
    return x * lax.rsqrt(jnp.mean(x * x, axis=-1, keepdims=True) + NORM_EPS) * g


def _dot(a, b):
    return jnp.dot(a, b, preferred_element_type=jnp.float32)


def _dot_nt(a, b):
    return lax.dot_general(a, b, _NT, preferred_element_type=jnp.float32)


def _resident(shape):
    nd = len(shape)
    return pl.BlockSpec(shape, lambda *_: (0,) * nd, pipeline_mode=pl.Buffered(1))


def _proj_kernel(x_ref, pos_ref, g1_ref, wa_ref, wq_ref, wkv_ref, wkrT_ref, wg_ref,
                 qng_ref, kvng_ref, wuqT_ref, wukp_ref, wuvT_ref, invf_ref,
                 u_ref, g_ref, qT_ref, k_ref, vT_ref, kstat_ref, *, q_scale):
    bf = jnp.bfloat16
    hb = _rms(x_ref[...], g1_ref[...]).astype(bf)

    qn = _rms(_dot(hb, wq_ref[...]), qng_ref[...]).astype(bf)
    kvn = _rms(_dot(hb, wkv_ref[...]), kvng_ref[...]).astype(bf)

    ang = invf_ref[...] * pos_ref[...].astype(jnp.float32)
    cos_t, sin_t = jnp.cos(ang), jnp.sin(ang)

    def rope_t(t1, t2):
        return t1 * cos_t - t2 * sin_t, t2 * cos_t + t1 * sin_t

    q_t = _dot_nt(wuqT_ref[...], qn)
    tm = q_t.shape[1]
    slot_pad = jnp.zeros((K_SLOT - QK_HEAD_DIM, tm), jnp.float32)
    for h in range(N_HEADS):
        r0 = h * QK_HEAD_DIM
        r1, r2 = rope_t(q_t[r0 + QK_NOPE_DIM:r0 + QK_NOPE_DIM + HALF_ROPE],
                        q_t[r0 + QK_NOPE_DIM + HALF_ROPE:r0 + QK_HEAD_DIM])
        blk = jnp.concatenate([q_t[r0:r0 + QK_NOPE_DIM] * q_scale, r1 * q_scale, r2 * q_scale, slot_pad],
                              axis=0)
        qT_ref[h * K_SLOT:(h + 1) * K_SLOT, :] = blk.astype(bf)

    kr_t = _dot_nt(wkrT_ref[...], hb)
    k1, k2 = rope_t(kr_t[:HALF_ROPE], kr_t[HALF_ROPE:])
    tail_rows = lax.broadcasted_iota(jnp.int32, (K_SLOT - QK_HEAD_DIM, tm), 0)
    kr_slot_t = jnp.concatenate(
        [jnp.zeros((QK_NOPE_DIM, tm), jnp.float32), k1, k2,
         jnp.where(tail_rows == 0, 1.0, 0.0)], axis=0)
    kr_slot = kr_slot_t.T
    knp = _dot(kvn, wukp_ref[...])
    head_rows = lax.broadcasted_iota(jnp.int32, (N_HEADS, LANES), 0)
    kstat = jnp.zeros((N_HEADS, LANES), jnp.float32)
    for h in range(N_HEADS):
        c0 = h * K_SLOT
        k_h = knp[:, c0:c0 + K_SLOT] + kr_slot
        k_ref[:, c0:c0 + K_SLOT] = k_h.astype(bf)
        norm2 = jnp.max(jnp.sum(k_h * k_h, axis=-1, keepdims=True), axis=0, keepdims=True)
        kstat = jnp.where(head_rows == h, norm2, kstat)
    kstat_ref[...] = kstat

    vT_ref[...] = _dot_nt(wuvT_ref[...], kvn).astype(bf)

    a = _dot(hb, wa_ref[...])
    u_ref[...] = a[:, :CONV_WIDTH] * jax.nn.sigmoid(a[:, CONV_WIDTH:])
    g_ref[...] = jax.nn.sigmoid(_dot(hb, wg_ref[...])).astype(bf)


def _conv_kernel(u_ref, up_ref, un_ref, g_ref, dww_ref, dwb_ref, lng_ref, lnb_ref,
                 wo_ref, bo_ref, c_ref, ext_ref, act_ref):
    i = pl.program_id(1)
    n = pl.num_programs(1)
    tm = u_ref.shape[0]
    n_lt = CONV_WIDTH // LANES
    lane = lambda l: slice(l * LANES, (l + 1) * LANES)
    for l in range(n_lt):
        ext_ref[l, 0:CONV_HALO, :] = jnp.where(i > 0, up_ref[:, lane(l)], 0.0)
        ext_ref[l, CONV_HALO:CONV_HALO + tm, :] = u_ref[:, lane(l)]
        ext_ref[l, CONV_HALO + tm:, :] = jnp.where(i < n - 1, un_ref[:, lane(l)], 0.0)

    half = CONV_ROWS // 2
    first = CONV_HALO - CONV_KERNEL // 2

    def rows_chunk(ci):
        r0 = ci * CONV_ROWS
        for par in range(2):
            ys = []
            for l in range(n_lt):
                acc = jnp.zeros((half, LANES), jnp.float32)
                for k in range(CONV_KERNEL):
                    tap = ext_ref[l, pl.ds(first + r0 + par + k, half, stride=2), :]
                    acc = acc + tap * dww_ref[k:k + 1, lane(l)]
                ys.append(acc)
            y = jnp.concatenate(ys, axis=1) + dwb_ref[...]
            mu = jnp.mean(y, axis=-1, keepdims=True)
            yc = y - mu
            yn = yc * lax.rsqrt(jnp.mean(yc * yc, axis=-1, keepdims=True) + NORM_EPS)
            yn = yn * lng_ref[...] + lnb_ref[...]
            a = yn * jax.nn.sigmoid(yn)
            for l in range(n_lt):
                act_ref[l, pl.ds(r0 + par, half, stride=2), :] = a[:, lane(l)]

    for ci in range(tm // CONV_ROWS):
        rows_chunk(ci)
    act = jnp.concatenate([act_ref[l] for l in range(n_lt)], axis=1).astype(jnp.bfloat16)
    y_conv = _dot(act, wo_ref[...]) + bo_ref[...]
    c_ref[...] = (g_ref[...].astype(jnp.float32) * y_conv).astype(jnp.bfloat16)


def _attn_chains(tq):
    n_sub = tq // TQ_SUB
    return [(h, qs) for h in range(HEADS_PER_STEP) for qs in range(n_sub)]


def _value_rows(vT_ref, h, r0):
    return jnp.concatenate([vT_ref[h * V_HEAD_DIM:(h + 1) * V_HEAD_DIM, pl.ds(r0, TK)],
                            jnp.ones((16, TK), jnp.bfloat16)], axis=0)


def _attn_running_max(qT_ref, k_ref, vT_ref, s_buf, acc_ref):
    bf = jnp.bfloat16
    n_chunks = k_ref.shape[0] // TK
    chains = _attn_chains(qT_ref.shape[1])

    def scores(j, slot):
        r0 = j * TK if isinstance(j, int) else pl.multiple_of(j * TK, TK)
        tops = []
        for c, (h, qs) in enumerate(chains):
            k_blk = k_ref[pl.ds(r0, TK), h * K_SLOT:(h + 1) * K_SLOT]
            q_t = qT_ref[h * K_SLOT:(h + 1) * K_SLOT, qs * TQ_SUB:(qs + 1) * TQ_SUB]
            s_t = _dot(k_blk, q_t)
            s_buf[slot, c] = s_t
            tops.append(jnp.max(s_t, axis=0, keepdims=True))
        return tuple(tops)

    def consume(j, slot, ms, tops):
        r0 = j * TK if isinstance(j, int) else pl.multiple_of(j * TK, TK)
        out = []
        for c, (h, qs) in enumerate(chains):
            m_new = jnp.maximum(ms[c], tops[c])
            p_t = jnp.exp2(s_buf[slot, c] - m_new).astype(bf)
            acc_ref[c] = jnp.exp2(ms[c] - m_new) * acc_ref[c] + _dot(_value_rows(vT_ref, h, r0), p_t)
            out.append(m_new)
        return tuple(out)

    acc_ref[...] = jnp.zeros_like(acc_ref)
    ms = tuple(jnp.full((1, TQ_SUB), -jnp.inf, jnp.float32) for _ in chains)
    tops = scores(0, 0)

    def group(first, carry, last=False):
        ms, tops = carry
        for u in range(ATTN_UNROLL):
            nxt = None if (last and u == ATTN_UNROLL - 1) else scores(first + u + 1, (u + 1) % 2)
            ms = consume(first + u, u % 2, ms, tops)
            tops = nxt
        return ms, tops

    n_groups = n_chunks // ATTN_UNROLL
    carry = lax.fori_loop(0, n_groups - 1, lambda i, cr: group(i * ATTN_UNROLL, cr), (ms, tops))
    group((n_groups - 1) * ATTN_UNROLL, carry, last=True)


def _attn_bounded(q_aug, k_ref, vT_ref, p_buf, acc_ref):
    n_chunks = k_ref.shape[0] // TK
    chains = _attn_chains(q_aug.shape[1])
    row0 = lambda j: j * TK if isinstance(j, int) else pl.multiple_of(j * TK, TK)

    def produce(j, slot, only=None):
        for c, (h, qs) in enumerate(chains):
            if only is not None and c != only:
                continue
            k_blk = k_ref[pl.ds(row0(j), TK), h * K_SLOT:(h + 1) * K_SLOT]
            q_t = q_aug[h * K_SLOT:(h + 1) * K_SLOT, qs * TQ_SUB:(qs + 1) * TQ_SUB]
            p_t = jnp.exp2(_dot(k_blk, q_t))
            p_buf[slot, c] = p_t.astype(jnp.bfloat16)
            acc_ref[c, V_HEAD_DIM:V_HEAD_DIM + 1, :] += jnp.sum(p_t, axis=0, keepdims=True)

    def consume(j, slot, only=None):
        for c, (h, qs) in enumerate(chains):
            if only is not None and c != only:
                continue
            v_t = vT_ref[h * V_HEAD_DIM:(h + 1) * V_HEAD_DIM, pl.ds(row0(j), TK)]
            acc_ref[c, 0:V_HEAD_DIM, :] += _dot(v_t, p_buf[slot, c])

    def group(first, last=False):
        for u in range(BOUNDED_UNROLL):
            for c in range(len(chains)):
                if not (last and u == BOUNDED_UNROLL - 1):
                    produce(first + u + 1, (u + 1) % 2, only=c)
                consume(first + u, u % 2, only=c)

    acc_ref[...] = jnp.zeros_like(acc_ref)
    produce(0, 0)
    n_groups = n_chunks // BOUNDED_UNROLL

    def body(i, carry):
        group(i * BOUNDED_UNROLL)
        return carry

    lax.fori_loop(0, n_groups - 1, body, 0)
    group((n_groups - 1) * BOUNDED_UNROLL, last=True)


def _attn_kernel(qT_ref, k_ref, vT_ref, kstat_ref, o_ref, q_aug, p_buf, s_buf, acc_ref):
    f32 = jnp.float32
    tq = qT_ref.shape[1]
    n_sub = tq // TQ_SUB
    hp = pl.program_id(1)
    key_norm2 = jnp.max(kstat_ref[...], axis=0)
    head_rows = lax.broadcasted_iota(jnp.int32, key_norm2.shape, 0)
    q_rows = lax.broadcasted_iota(jnp.int32, (K_SLOT, tq), 0)
    for h in range(HEADS_PER_STEP):
        q = qT_ref[h * K_SLOT:(h + 1) * K_SLOT, :].astype(f32)
        k2 = jnp.max(jnp.where(head_rows == HEADS_PER_STEP * hp + h, key_norm2, 0.0),
                     axis=0, keepdims=True)
        k2 = jnp.concatenate([k2] * (tq // LANES), axis=1)
        bound = jnp.sqrt(jnp.sum(q * q, axis=0, keepdims=True) * k2) * BOUND_SLACK
        q_aug[h * K_SLOT:(h + 1) * K_SLOT, :] = jnp.where(
            q_rows == QK_HEAD_DIM, -bound, q).astype(jnp.bfloat16)

    _attn_bounded(q_aug, k_ref, vT_ref, p_buf, acc_ref)
    denom_min = jnp.min(acc_ref[:, V_HEAD_DIM:V_HEAD_DIM + 1, :])
    trusted = denom_min > MIN_TRUSTED_DENOM

    @pl.when(jnp.logical_not(trusted))
    def _():
        _attn_running_max(qT_ref, k_ref, vT_ref, s_buf, acc_ref)

    for qs in range(n_sub):
        o_t = jnp.concatenate(
            [acc_ref[h * n_sub + qs, :V_HEAD_DIM] / acc_ref[h * n_sub + qs, V_HEAD_DIM:V_HEAD_DIM + 1]
             for h in range(HEADS_PER_STEP)], axis=0)
        o_ref[qs * TQ_SUB:(qs + 1) * TQ_SUB, :] = o_t.T.astype(jnp.bfloat16)


def _merge_kernel(o_ref, c_ref, g_ref, x_ref, wm_ref, wo_ref, n2_ref, x1_ref, h2_ref):
    y_mla = _dot(o_ref[...], wm_ref[...])
    merged = c_ref[...].astype(jnp.float32) + g_ref[...].astype(jnp.float32) * y_mla
    x1 = x_ref[...] + _dot(merged.astype(jnp.bfloat16), wo_ref[...])
    x1_ref[...] = x1
    h2_ref[...] = _rms(x1, n2_ref[...]).astype(jnp.bfloat16)


def _ffn_kernel(h_ref, hp_ref, hn_ref, x1_ref, wup_ref, dww_ref, dwb_ref, wdn_ref, nf_ref,
                out_ref, hext_ref, z_buf, act_ref, y_ref):
    i = pl.program_id(1)
    n = pl.num_programs(1)
    tm = h_ref.shape[0]
    half = tm // 2
    n_c = D_FF // FFN_CHUNK
    n_lt = FFN_CHUNK // LANES
    lane = lambda l: slice(l * LANES, (l + 1) * LANES)
    hext_ref[0:FFN_HALO, :] = jnp.where(i > 0, hp_ref[...], jnp.zeros_like(hp_ref))
    hext_ref[FFN_HALO:FFN_HALO + tm, :] = h_ref[...]
    hext_ref[FFN_HALO + tm:, :] = jnp.where(i < n - 1, hn_ref[...], jnp.zeros_like(hn_ref))

    def up_proj(c):
        for part in range(2):
            c0 = part * D_FF + c * FFN_CHUNK
            z = _dot(hext_ref[...], wup_ref[:, c0:c0 + FFN_CHUNK])
            for l in range(n_lt):
                z_buf[c % 2, part, l] = z[:, lane(l)]

    def conv3(c, part, par):
        ys = []
        for l in range(n_lt):
            c0 = part * D_FF + c * FFN_CHUNK + l * LANES
            w = dww_ref[:, c0:c0 + LANES]
            taps = [z_buf[c % 2, part, l, pl.ds(FFN_HALO + par + k - 1, half, stride=2), :]
                    for k in range(FFN_KERNEL)]
            ys.append(taps[0] * w[0:1] + taps[1] * w[1:2] + taps[2] * w[2:3]
                      + dwb_ref[:, c0:c0 + LANES])
        return jnp.concatenate(ys, axis=1)

    def activate(c):
        for par in range(2):
            gate, up = conv3(c, 0, par), conv3(c, 1, par)
            act_ref[par * half:(par + 1) * half, c * FFN_CHUNK:(c + 1) * FFN_CHUNK] = (
                gate * jax.nn.sigmoid(gate) * up).astype(jnp.bfloat16)

    def down_proj(c):
        return _dot(act_ref[:, c * FFN_CHUNK:(c + 1) * FFN_CHUNK],
                    wdn_ref[c * FFN_CHUNK:(c + 1) * FFN_CHUNK, :])

    up_proj(0)
    y = None
    for c in range(n_c):
        if c + 1 < n_c:
            up_proj(c + 1)
        if c >= 1:
            y = down_proj(c - 1) if y is None else y + down_proj(c - 1)
        activate(c)
    y = y + down_proj(n_c - 1)
    for par in range(2):
        for l in range(D_MODEL // LANES):
            y_ref[l, pl.ds(par, half, stride=2), :] = y[par * half:(par + 1) * half, lane(l)]
    y_tok = jnp.concatenate([y_ref[l] for l in range(D_MODEL // LANES)], axis=1)
    out_ref[...] = _rms(x1_ref[...] + y_tok, nf_ref[...])


def _layer(x, positions, norm1_g, w_in, conv_dw_w, conv_dw_b, conv_ln_g, conv_ln_b,
           w_conv_out, b_conv_out, q_norm_g, w_uq, kv_norm_g, w_ukv, w_mla_out, w_out,
           norm2_g, w_ffn_up, ffn_dw_w, ffn_dw_b, w_ffn_down, norm_f_g):
    bf = jnp.bfloat16
    f32 = jnp.float32
    B, S, D = x.shape
    assert D == D_MODEL and S % TM == 0 and S % TQ == 0 and S % TK == 0
    n_t = S // TM
    row = lambda v: v.reshape(1, -1).astype(f32)
    params = functools.partial(pltpu.CompilerParams, vmem_limit_bytes=VMEM_LIMIT)

    o_q = 2 * CONV_WIDTH
    o_kv = o_q + Q_LORA_RANK
    o_kr = o_kv + KV_LORA_RANK
    o_g = o_kr + QK_ROPE_DIM
    wa = w_in[:, :o_q].astype(bf)
    wq = w_in[:, o_q:o_kv].astype(bf)
    wkv = w_in[:, o_kv:o_kr].astype(bf)
    wkrT = w_in[:, o_kr:o_g].T.astype(bf)
    wg = w_in[:, o_g:].astype(bf)
    wuqT = w_uq.T.astype(bf)
    w_ukv_h = w_ukv.reshape(KV_LORA_RANK, N_HEADS, QK_NOPE_DIM + V_HEAD_DIM)
    wukp = jnp.pad(w_ukv_h[:, :, :QK_NOPE_DIM], ((0, 0), (0, 0), (0, K_SLOT - QK_NOPE_DIM)))
    wukp = wukp.reshape(KV_LORA_RANK, N_HEADS * K_SLOT).astype(bf)
    wuvT = w_ukv_h[:, :, QK_NOPE_DIM:].reshape(KV_LORA_RANK, MLA_WIDTH).T.astype(bf)
    inv_freq = 1.0 / (ROPE_THETA ** (jnp.arange(0, QK_ROPE_DIM, 2, dtype=f32) / QK_ROPE_DIM))
    q_scale = (QK_HEAD_DIM ** -0.5) * math.log2(math.e)

    tok = lambda w: pl.BlockSpec((None, TM, w), lambda b, i: (b, i, 0))
    tok_t = lambda r: pl.BlockSpec((None, r, TM), lambda b, i: (b, 0, i))

    u, g, q_t, k, v_t, kstat = pl.pallas_call(
        functools.partial(_proj_kernel, q_scale=q_scale),
        grid=(B, n_t),
        in_specs=[tok(D), pl.BlockSpec((None, 1, TM), lambda b, i: (b, 0, i)),
                  _resident((1, D)), _resident(wa.shape), _resident(wq.shape), _resident(wkv.shape),
                  _resident(wkrT.shape), _resident(wg.shape), _resident((1, Q_LORA_RANK)),
                  _resident((1, KV_LORA_RANK)), _resident(wuqT.shape), _resident(wukp.shape),
                  _resident(wuvT.shape), _resident((HALF_ROPE, 1))],
        out_specs=[tok(CONV_WIDTH), tok(2 * D), tok_t(N_HEADS * K_SLOT),
                   tok(N_HEADS * K_SLOT), tok_t(MLA_WIDTH),
                   pl.BlockSpec((None, None, N_HEADS, LANES), lambda b, i: (b, i, 0, 0))],
        out_shape=[jax.ShapeDtypeStruct((B, S, CONV_WIDTH), f32),
                   jax.ShapeDtypeStruct((B, S, 2 * D), bf),
                   jax.ShapeDtypeStruct((B, N_HEADS * K_SLOT, S), bf),
                   jax.ShapeDtypeStruct((B, S, N_HEADS * K_SLOT), bf),
                   jax.ShapeDtypeStruct((B, MLA_WIDTH, S), bf),
                   jax.ShapeDtypeStruct((B, n_t, N_HEADS, LANES), f32)],
        compiler_params=params(dimension_semantics=("arbitrary", "arbitrary")),
        name="proj",
    )(x, positions.reshape(B, 1, S), row(norm1_g), wa, wq, wkv, wkrT, wg, row(q_norm_g),
      row(kv_norm_g), wuqT, wukp, wuvT, inv_freq.reshape(HALF_ROPE, 1))

    hb = TM // CONV_HALO
    n_hb = S // CONV_HALO
    c = pl.pallas_call(
        _conv_kernel,
        grid=(B, n_t),
        in_specs=[tok(CONV_WIDTH),
                  pl.BlockSpec((None, CONV_HALO, CONV_WIDTH),
                               lambda b, i: (b, jnp.maximum(i * hb - 1, 0), 0)),
                  pl.BlockSpec((None, CONV_HALO, CONV_WIDTH),
                               lambda b, i: (b, jnp.minimum((i + 1) * hb, n_hb - 1), 0)),
                  tok(D),
                  _resident((CONV_KERNEL, CONV_WIDTH)), _resident((1, CONV_WIDTH)),
                  _resident((1, CONV_WIDTH)), _resident((1, CONV_WIDTH)),
                  _resident((CONV_WIDTH, D)), _resident((1, D))],
        out_specs=tok(D),
        out_shape=jax.ShapeDtypeStruct((B, S, D), bf),
        scratch_shapes=[pltpu.VMEM((CONV_WIDTH // LANES, TM + 2 * CONV_HALO, LANES), f32),
                        pltpu.VMEM((CONV_WIDTH // LANES, TM, LANES), f32)],
        compiler_params=params(dimension_semantics=("arbitrary", "arbitrary")),
        name="conv",
    )(u, u, u, g, conv_dw_w.astype(f32), row(conv_dw_b), row(conv_ln_g), row(conv_ln_b),
      w_conv_out.astype(bf), row(b_conv_out))

    n_hp = N_HEADS // HEADS_PER_STEP
    o = pl.pallas_call(
        _attn_kernel,
        grid=(B, n_hp, S // TQ),
        in_specs=[pl.BlockSpec((None, HEADS_PER_STEP * K_SLOT, TQ), lambda b, h, i: (b, h, i)),
                  pl.BlockSpec((None, S, HEADS_PER_STEP * K_SLOT), lambda b, h, i: (b, 0, h)),
                  pl.BlockSpec((None, HEADS_PER_STEP * V_HEAD_DIM, S), lambda b, h, i: (b, h, 0)),
                  pl.BlockSpec((None, n_t, N_HEADS, LANES), lambda b, h, i: (b, 0, 0, 0))],
        out_specs=pl.BlockSpec((None, TQ, HEADS_PER_STEP * V_HEAD_DIM), lambda b, h, i: (b, i, h)),
        out_shape=jax.ShapeDtypeStruct((B, S, MLA_WIDTH), bf),
        scratch_shapes=[
            pltpu.VMEM((HEADS_PER_STEP * K_SLOT, TQ), bf),
            pltpu.VMEM((2, HEADS_PER_STEP * (TQ // TQ_SUB), TK, TQ_SUB), bf),
            pltpu.VMEM((2, HEADS_PER_STEP * (TQ // TQ_SUB), TK, TQ_SUB), f32),
            pltpu.VMEM((HEADS_PER_STEP * (TQ // TQ_SUB), V_HEAD_DIM + 16, TQ_SUB), f32)],
        compiler_params=params(dimension_semantics=("arbitrary", "arbitrary", "arbitrary")),
        name="attn",
    )(q_t, k, v_t, kstat)

    mtok = lambda w: pl.BlockSpec((None, MERGE_TM, w), lambda b, i: (b, i, 0))
    x1, h2 = pl.pallas_call(
        _merge_kernel,
        grid=(B, S // MERGE_TM),
        in_specs=[mtok(MLA_WIDTH), mtok(D),
                  pl.BlockSpec((None, MERGE_TM, D), lambda b, i: (b, i, 1)),
                  mtok(D), _resident((MLA_WIDTH, D)), _resident((D, D)), _resident((1, D))],
        out_specs=[mtok(D), mtok(D)],
        out_shape=[jax.ShapeDtypeStruct((B, S, D), f32), jax.ShapeDtypeStruct((B, S, D), bf)],
        compiler_params=params(dimension_semantics=("arbitrary", "arbitrary")),
        name="merge",
    )(o, c, g, x, w_mla_out.astype(bf), w_out.astype(bf), row(norm2_g))

    fb = TM // FFN_HALO
    n_fb = S // FFN_HALO
    out = pl.pallas_call(
        _ffn_kernel,
        grid=(B, n_t),
        in_specs=[tok(D),
                  pl.BlockSpec((None, FFN_HALO, D), lambda b, i: (b, jnp.maximum(i * fb - 1, 0), 0)),
                  pl.BlockSpec((None, FFN_HALO, D),
                               lambda b, i: (b, jnp.minimum((i + 1) * fb, n_fb - 1), 0)),
                  tok(D), _resident((D, 2 * D_FF)), _resident((FFN_KERNEL, 2 * D_FF)),
                  _resident((1, 2 * D_FF)), _resident((D_FF, D)), _resident((1, D))],
        out_specs=tok(D),
        out_shape=jax.ShapeDtypeStruct((B, S, D), f32),
        scratch_shapes=[pltpu.VMEM((TM + 2 * FFN_HALO, D), bf),
                        pltpu.VMEM((2, 2, FFN_CHUNK // LANES, TM + 2 * FFN_HALO, LANES), f32),
                        pltpu.VMEM((TM, D_FF), bf),
                        pltpu.VMEM((D // LANES, TM, LANES), f32)],
        compiler_params=params(dimension_semantics=("arbitrary", "arbitrary")),
        name="ffn",
    )(h2, h2, h2, x1, w_ffn_up.astype(bf), ffn_dw_w.astype(f32), row(ffn_dw_b),
      w_ffn_down.astype(bf), row(norm_f_g))
    return out


def kernel(x, positions, norm1_g, w_in, conv_dw_w, conv_dw_b, conv_ln_g, conv_ln_b, w_conv_out,
           b_conv_out, q_norm_g, w_uq, kv_norm_g, w_ukv, w_mla_out, w_out, norm2_g, w_ffn_up,
           ffn_dw_w, ffn_dw_b, w_ffn_down, norm_f_g):
    assert norm1_g.shape[0] == 1, "single-layer block"
    return _layer(x, positions, norm1_g[0], w_in[0], conv_dw_w[0], conv_dw_b[0], conv_ln_g[0],
                  conv_ln_b[0], w_conv_out[0], b_conv_out[0], q_norm_g[0], w_uq[0], kv_norm_g[0],
                  w_ukv[0], w_mla_out[0], w_out[0], norm2_g[0], w_ffn_up[0], ffn_dw_w[0],
                  ffn_dw_b[0], w_ffn_down[0], norm_f_g)
```

```python
import functools
import math

import jax
import jax.numpy as jnp
from jax import lax
from jax.experimental import pallas as pl
from jax.experimental.pallas import tpu as pltpu

D_MODEL = 1024
CONV_WIDTH = 512
CONV_KERNEL = 31
N_HEADS = 8
QK_NOPE_DIM = 64
QK_ROPE_DIM = 32
V_HEAD_DIM = 64
Q_LORA_RANK = 384
KV_LORA_RANK = 256
ROPE_THETA = 10000.0
QK_HEAD_DIM = QK_NOPE_DIM + QK_ROPE_DIM
MLA_WIDTH = N_HEADS * V_HEAD_DIM
D_FF = 2816
FFN_KERNEL = 3
NORM_EPS = 1e-6

LANES = 128
HALF_ROPE = QK_ROPE_DIM // 2
K_SLOT = LANES
HEADS_PER_STEP = 2
VMEM_LIMIT = 56 * 1024 * 1024

TM = 512
MERGE_TM = 1024
CONV_HALO = 16
CONV_ROWS = 128
FFN_HALO = 16
FFN_CHUNK = 256
TQ = 1024
TQ_SUB = 512
TK = 256
ATTN_UNROLL = 4
BOUNDED_UNROLL = 16
BOUND_SLACK = 1.01
MIN_TRUSTED_DENOM = 2.0 ** -60

_NT = (((1,), (1,)), ((), ()))


def _rms(x, g):
    return x * lax.rsqrt(jnp.mean(x * x, axis=-1, keepdims=True) + NORM_EPS) * g


def _dot(a, b):
    return jnp.dot(a, b, preferred_element_type=jnp.float32)


def _dot_nt(a, b):
    return lax.dot_general(a, b, _NT, preferred_element_type=jnp.float32)


def _resident(shape):
    nd = len(shape)
    return pl.BlockSpec(shape, lambda *_: (0,) * nd, pipeline_mode=pl.Buffered(1))


def _proj_kernel(x_ref, pos_ref, g1_ref, win_ref, wkrT_ref, wg_ref,
                 qng_ref, kvng_ref, wuqT_ref, wukp_ref, wuvT_ref, invf_ref,
                 u_ref, g_ref, qT_ref, k_ref, vT_ref, kstat_ref, *, q_scale):
    bf = jnp.bfloat16
    hb = _rms(x_ref[...], g1_ref[...]).astype(bf)

    o_q = 2 * CONV_WIDTH
    o_kv = o_q + Q_LORA_RANK
    o_kr = o_kv + KV_LORA_RANK
    qn = _rms(_dot(hb, win_ref[:, o_q:o_kv]), qng_ref[...]).astype(bf)
    kvn = _rms(_dot(hb, win_ref[:, o_kv:o_kr]), kvng_ref[...]).astype(bf)

    ang = invf_ref[...] * pos_ref[...].astype(jnp.float32)
    cos_t, sin_t = jnp.cos(ang), jnp.sin(ang)

    def rope_t(t1, t2):
        return t1 * cos_t - t2 * sin_t, t2 * cos_t + t1 * sin_t

    q_t = _dot_nt(wuqT_ref[...], qn)
    tm = q_t.shape[1]
    slot_pad = jnp.zeros((K_SLOT - QK_HEAD_DIM, tm), jnp.float32)
    for h in range(N_HEADS):
        r0 = h * QK_HEAD_DIM
        r1, r2 = rope_t(q_t[r0 + QK_NOPE_DIM:r0 + QK_NOPE_DIM + HALF_ROPE],
                        q_t[r0 + QK_NOPE_DIM + HALF_ROPE:r0 + QK_HEAD_DIM])
        blk = jnp.concatenate([q_t[r0:r0 + QK_NOPE_DIM] * q_scale, r1 * q_scale, r2 * q_scale, slot_pad],
                              axis=0)
        qT_ref[h * K_SLOT:(h + 1) * K_SLOT, :] = blk.astype(bf)

    kr_t = _dot_nt(wkrT_ref[...], hb)
    k1, k2 = rope_t(kr_t[:HALF_ROPE], kr_t[HALF_ROPE:])
    tail_rows = lax.broadcasted_iota(jnp.int32, (K_SLOT - QK_HEAD_DIM, tm), 0)
    kr_slot_t = jnp.concatenate(
        [jnp.zeros((QK_NOPE_DIM, tm), jnp.float32), k1, k2,
         jnp.where(tail_rows == 0, 1.0, 0.0)], axis=0)
    kr_slot = kr_slot_t.T
    knp = _dot(kvn, wukp_ref[...])
    head_rows = lax.broadcasted_iota(jnp.int32, (N_HEADS, LANES), 0)
    kstat = jnp.zeros((N_HEADS, LANES), jnp.float32)
    for h in range(N_HEADS):
        c0 = h * K_SLOT
        k_h = knp[:, c0:c0 + K_SLOT] + kr_slot
        k_ref[:, c0:c0 + K_SLOT] = k_h.astype(bf)
        norm2 = jnp.max(jnp.sum(k_h * k_h, axis=-1, keepdims=True), axis=0, keepdims=True)
        kstat = jnp.where(head_rows == h, norm2, kstat)
    kstat_ref[...] = kstat

    vT_ref[...] = _dot_nt(wuvT_ref[...], kvn).astype(bf)

    a = _dot(hb, win_ref[:, :o_q])
    u_ref[...] = a[:, :CONV_WIDTH] * jax.nn.sigmoid(a[:, CONV_WIDTH:])
    g_ref[...] = jax.nn.sigmoid(_dot(hb, wg_ref[...])).astype(bf)


def _conv_kernel(u_ref, up_ref, un_ref, g_ref, dww_ref, dwb_ref, lng_ref, lnb_ref,
                 wo_ref, bo_ref, c_ref, ext_ref, act_ref):
    i = pl.program_id(1)
    n = pl.num_programs(1)
    tm = u_ref.shape[0]
    n_lt = CONV_WIDTH // LANES
    lane = lambda l: slice(l * LANES, (l + 1) * LANES)
    for l in range(n_lt):
        ext_ref[l, 0:CONV_HALO, :] = jnp.where(i > 0, up_ref[:, lane(l)], 0.0)
        ext_ref[l, CONV_HALO:CONV_HALO + tm, :] = u_ref[:, lane(l)]
        ext_ref[l, CONV_HALO + tm:, :] = jnp.where(i < n - 1, un_ref[:, lane(l)], 0.0)

    half = CONV_ROWS // 2
    first = CONV_HALO - CONV_KERNEL // 2

    def rows_chunk(ci):
        r0 = ci * CONV_ROWS
        for par in range(2):
            ys = []
            for l in range(n_lt):
                acc = jnp.zeros((half, LANES), jnp.float32)
                for k in range(CONV_KERNEL):
                    tap = ext_ref[l, pl.ds(first + r0 + par + k, half, stride=2), :]
                    acc = acc + tap * dww_ref[k:k + 1, lane(l)]
                ys.append(acc)
            y = jnp.concatenate(ys, axis=1) + dwb_ref[...]
            mu = jnp.mean(y, axis=-1, keepdims=True)
            yc = y - mu
            yn = yc * lax.rsqrt(jnp.mean(yc * yc, axis=-1, keepdims=True) + NORM_EPS)
            yn = yn * lng_ref[...] + lnb_ref[...]
            a = yn * jax.nn.sigmoid(yn)
            for l in range(n_lt):
                act_ref[l, pl.ds(r0 + par, half, stride=2), :] = a[:, lane(l)]

    for ci in range(tm // CONV_ROWS):
        rows_chunk(ci)
    act = jnp.concatenate([act_ref[l] for l in range(n_lt)], axis=1).astype(jnp.bfloat16)
    y_conv = _dot(act, wo_ref[...]) + bo_ref[...]
    c_ref[...] = (g_ref[...].astype(jnp.float32) * y_conv).astype(jnp.bfloat16)


def _attn_chains(tq):
    n_sub = tq // TQ_SUB
    return [(h, qs) for h in range(HEADS_PER_STEP) for qs in range(n_sub)]


def _value_rows(vT_ref, h, r0):
    return jnp.concatenate([vT_ref[h * V_HEAD_DIM:(h + 1) * V_HEAD_DIM, pl.ds(r0, TK)],
                            jnp.ones((16, TK), jnp.bfloat16)], axis=0)


def _attn_running_max(qT_ref, k_ref, vT_ref, s_buf, acc_ref):
    bf = jnp.bfloat16
    n_chunks = k_ref.shape[0] // TK
    chains = _attn_chains(qT_ref.shape[1])

    def scores(j, slot):
        r0 = j * TK if isinstance(j, int) else pl.multiple_of(j * TK, TK)
        tops = []
        for c, (h, qs) in enumerate(chains):
            k_blk = k_ref[pl.ds(r0, TK), h * K_SLOT:(h + 1) * K_SLOT]
            q_t = qT_ref[h * K_SLOT:(h + 1) * K_SLOT, qs * TQ_SUB:(qs + 1) * TQ_SUB]
            s_t = _dot(k_blk, q_t)
            s_buf[slot, c] = s_t
            tops.append(jnp.max(s_t, axis=0, keepdims=True))
        return tuple(tops)

    def consume(j, slot, ms, tops):
        r0 = j * TK if isinstance(j, int) else pl.multiple_of(j * TK, TK)
        out = []
        for c, (h, qs) in enumerate(chains):
            m_new = jnp.maximum(ms[c], tops[c])
            p_t = jnp.exp2(s_buf[slot, c] - m_new).astype(bf)
            acc_ref[c] = jnp.exp2(ms[c] - m_new) * acc_ref[c] + _dot(_value_rows(vT_ref, h, r0), p_t)
            out.append(m_new)
        return tuple(out)

    acc_ref[...] = jnp.zeros_like(acc_ref)
    ms = tuple(jnp.full((1, TQ_SUB), -jnp.inf, jnp.float32) for _ in chains)
    tops = scores(0, 0)

    def group(first, carry, last=False):
        ms, tops = carry
        for u in range(ATTN_UNROLL):
            nxt = None if (last and u == ATTN_UNROLL - 1) else scores(first + u + 1, (u + 1) % 2)
            ms = consume(first + u, u % 2, ms, tops)
            tops = nxt
        return ms, tops

    n_groups = n_chunks // ATTN_UNROLL
    carry = lax.fori_loop(0, n_groups - 1, lambda i, cr: group(i * ATTN_UNROLL, cr), (ms, tops))
    group((n_groups - 1) * ATTN_UNROLL, carry, last=True)


def _attn_bounded(q_aug, k_ref, vT_ref, p_buf, acc_ref):
    n_chunks = k_ref.shape[0] // TK
    chains = _attn_chains(q_aug.shape[1])
    row0 = lambda j: j * TK if isinstance(j, int) else pl.multiple_of(j * TK, TK)

    def produce(j, slot, c):
        h, qs = chains[c]
        k_blk = k_ref[pl.ds(row0(j), TK), h * K_SLOT:(h + 1) * K_SLOT]
        q_t = q_aug[h * K_SLOT:(h + 1) * K_SLOT, qs * TQ_SUB:(qs + 1) * TQ_SUB]
        p_t = jnp.exp2(_dot(k_blk, q_t))
        p_buf[slot, c] = p_t.astype(jnp.bfloat16)
        acc_ref[c, V_HEAD_DIM:V_HEAD_DIM + 1, :] += jnp.sum(p_t, axis=0, keepdims=True)

    def consume(j, slot, c):
        h, _ = chains[c]
        v_t = vT_ref[h * V_HEAD_DIM:(h + 1) * V_HEAD_DIM, pl.ds(row0(j), TK)]
        acc_ref[c, 0:V_HEAD_DIM, :] += _dot(v_t, p_buf[slot, c])

    def group(first, last=False):
        for u in range(BOUNDED_UNROLL):
            for c in range(len(chains)):
                if not (last and u == BOUNDED_UNROLL - 1):
                    produce(first + u + 1, (u + 1) % 2, c)
                consume(first + u, u % 2, c)

    acc_ref[...] = jnp.zeros_like(acc_ref)
    for c in range(len(chains)):
        produce(0, 0, c)
    n_groups = n_chunks // BOUNDED_UNROLL

    def body(i, carry):
        group(i * BOUNDED_UNROLL)
        return carry

    lax.fori_loop(0, n_groups - 1, body, 0)
    group((n_groups - 1) * BOUNDED_UNROLL, last=True)


def _attn_kernel(qT_ref, k_ref, vT_ref, kstat_ref, o_ref, q_aug, p_buf, s_buf, acc_ref):
    f32 = jnp.float32
    tq = qT_ref.shape[1]
    n_sub = tq // TQ_SUB
    hp = pl.program_id(1)
    key_norm2 = jnp.max(kstat_ref[...], axis=0)
    head_rows = lax.broadcasted_iota(jnp.int32, key_norm2.shape, 0)
    q_rows = lax.broadcasted_iota(jnp.int32, (K_SLOT, tq), 0)
    for h in range(HEADS_PER_STEP):
        q = qT_ref[h * K_SLOT:(h + 1) * K_SLOT, :].astype(f32)
        k2 = jnp.max(jnp.where(head_rows == HEADS_PER_STEP * hp + h, key_norm2, 0.0),
                     axis=0, keepdims=True)
        k2 = jnp.concatenate([k2] * (tq // LANES), axis=1)
        bound = jnp.sqrt(jnp.sum(q * q, axis=0, keepdims=True) * k2) * BOUND_SLACK
        q_aug[h * K_SLOT:(h + 1) * K_SLOT, :] = jnp.where(
            q_rows == QK_HEAD_DIM, -bound, q).astype(jnp.bfloat16)

    _attn_bounded(q_aug, k_ref, vT_ref, p_buf, acc_ref)
    denom_min = jnp.min(acc_ref[:, V_HEAD_DIM:V_HEAD_DIM + 1, :])
    trusted = denom_min > MIN_TRUSTED_DENOM

    @pl.when(jnp.logical_not(trusted))
    def _():
        _attn_running_max(qT_ref, k_ref, vT_ref, s_buf, acc_ref)

    for qs in range(n_sub):
        o_t = jnp.concatenate(
            [acc_ref[h * n_sub + qs, :V_HEAD_DIM] / acc_ref[h * n_sub + qs, V_HEAD_DIM:V_HEAD_DIM + 1]
             for h in range(HEADS_PER_STEP)], axis=0)
        o_ref[qs * TQ_SUB:(qs + 1) * TQ_SUB, :] = o_t.T.astype(jnp.bfloat16)


def _merge_kernel(o_ref, c_ref, g_ref, x_ref, wm_ref, wo_ref, n2_ref, x1_ref, h2_ref):
    y_mla = _dot(o_ref[...], wm_ref[...])
    merged = c_ref[...].astype(jnp.float32) + g_ref[...].astype(jnp.float32) * y_mla
    x1 = x_ref[...] + _dot(merged.astype(jnp.bfloat16), wo_ref[...])
    x1_ref[...] = x1
    h2_ref[...] = _rms(x1, n2_ref[...]).astype(jnp.bfloat16)


def _ffn_kernel(h_ref, hp_ref, hn_ref, x1_ref, wup_ref, dww_ref, dwb_ref, wdn_ref, nf_ref,
                out_ref, hext_ref, z_buf, act_ref, y_ref):
    i = pl.program_id(1)
    n = pl.num_programs(1)
    tm = h_ref.shape[0]
    half = tm // 2
    n_c = D_FF // FFN_CHUNK
    n_lt = FFN_CHUNK // LANES
    lane = lambda l: slice(l * LANES, (l + 1) * LANES)
    hext_ref[0:FFN_HALO, :] = jnp.where(i > 0, hp_ref[...], jnp.zeros_like(hp_ref))
    hext_ref[FFN_HALO:FFN_HALO + tm, :] = h_ref[...]
    hext_ref[FFN_HALO + tm:, :] = jnp.where(i < n - 1, hn_ref[...], jnp.zeros_like(hn_ref))

    def up_proj(c):
        for part in range(2):
            c0 = part * D_FF + c * FFN_CHUNK
            z = _dot(hext_ref[...], wup_ref[:, c0:c0 + FFN_CHUNK])
            for l in range(n_lt):
                z_buf[c % 2, part, l] = z[:, lane(l)]

    def conv3(c, part, par):
        ys = []
        for l in range(n_lt):
            c0 = part * D_FF + c * FFN_CHUNK + l * LANES
            w = dww_ref[:, c0:c0 + LANES]
            taps = [z_buf[c % 2, part, l, pl.ds(FFN_HALO + par + k - 1, half, stride=2), :]
                    for k in range(FFN_KERNEL)]
            ys.append(taps[0] * w[0:1] + taps[1] * w[1:2] + taps[2] * w[2:3]
                      + dwb_ref[:, c0:c0 + LANES])
        return jnp.concatenate(ys, axis=1)

    def activate(c):
        for par in range(2):
            gate, up = conv3(c, 0, par), conv3(c, 1, par)
            act_ref[par * half:(par + 1) * half, c * FFN_CHUNK:(c + 1) * FFN_CHUNK] = (
                gate * jax.nn.sigmoid(gate) * up).astype(jnp.bfloat16)

    def down_proj(c):
        return _dot(act_ref[:, c * FFN_CHUNK:(c + 1) * FFN_CHUNK],
                    wdn_ref[c * FFN_CHUNK:(c + 1) * FFN_CHUNK, :])

    up_proj(0)
    y = None
    for c in range(n_c):
        if c + 1 < n_c:
            up_proj(c + 1)
        if c >= 1:
            y = down_proj(c - 1) if y is None else y + down_proj(c - 1)
        activate(c)
    y = y + down_proj(n_c - 1)
    for par in range(2):
        for l in range(D_MODEL // LANES):
            y_ref[l, pl.ds(par, half, stride=2), :] = y[par * half:(par + 1) * half, lane(l)]
    y_tok = jnp.concatenate([y_ref[l] for l in range(D_MODEL // LANES)], axis=1)
    out_ref[...] = _rms(x1_ref[...] + y_tok, nf_ref[...])


def _layer(x, positions, norm1_g, w_in, conv_dw_w, conv_dw_b, conv_ln_g, conv_ln_b,
           w_conv_out, b_conv_out, q_norm_g, w_uq, kv_norm_g, w_ukv, w_mla_out, w_out,
           norm2_g, w_ffn_up, ffn_dw_w, ffn_dw_b, w_ffn_down, norm_f_g):
    bf = jnp.bfloat16
    f32 = jnp.float32
    B, S, D = x.shape
    assert D == D_MODEL and S % TM == 0 and S % MERGE_TM == 0 and S % TQ == 0
    assert (S // TK) % ATTN_UNROLL == 0 and (S // TK) % BOUNDED_UNROLL == 0 and TQ % TQ_SUB == 0
    n_t = S // TM
    row = lambda v: v.reshape(1, -1).astype(f32)
    params = functools.partial(pltpu.CompilerParams, vmem_limit_bytes=VMEM_LIMIT)

    o_q = 2 * CONV_WIDTH
    o_kv = o_q + Q_LORA_RANK
    o_kr = o_kv + KV_LORA_RANK
    o_g = o_kr + QK_ROPE_DIM
    w_lead = w_in[:, :o_kr].astype(bf)
    wkrT = w_in[:, o_kr:o_g].T.astype(bf)
    wg = w_in[:, o_g:].astype(bf)
    wuqT = w_uq.T.astype(bf)
    w_ukv_h = w_ukv.reshape(KV_LORA_RANK, N_HEADS, QK_NOPE_DIM + V_HEAD_DIM)
    wukp = jnp.pad(w_ukv_h[:, :, :QK_NOPE_DIM], ((0, 0), (0, 0), (0, K_SLOT - QK_NOPE_DIM)))
    wukp = wukp.reshape(KV_LORA_RANK, N_HEADS * K_SLOT).astype(bf)
    wuvT = w_ukv_h[:, :, QK_NOPE_DIM:].reshape(KV_LORA_RANK, MLA_WIDTH).T.astype(bf)
    inv_freq = 1.0 / (ROPE_THETA ** (jnp.arange(0, QK_ROPE_DIM, 2, dtype=f32) / QK_ROPE_DIM))
    q_scale = (QK_HEAD_DIM ** -0.5) * math.log2(math.e)

    tok = lambda w: pl.BlockSpec((None, TM, w), lambda b, i: (b, i, 0))
    tok_t = lambda r: pl.BlockSpec((None, r, TM), lambda b, i: (b, 0, i))

    u, g, q_t, k, v_t, kstat = pl.pallas_call(
        functools.partial(_proj_kernel, q_scale=q_scale),
        grid=(B, n_t),
        in_specs=[tok(D), pl.BlockSpec((None, 1, TM), lambda b, i: (b, 0, i)),
                  _resident((1, D)), _resident(w_lead.shape), _resident(wkrT.shape), _resident(wg.shape), _resident((1, Q_LORA_RANK)),
                  _resident((1, KV_LORA_RANK)), _resident(wuqT.shape), _resident(wukp.shape),
                  _resident(wuvT.shape), _resident((HALF_ROPE, 1))],
        out_specs=[tok(CONV_WIDTH), tok(2 * D), tok_t(N_HEADS * K_SLOT),
                   tok(N_HEADS * K_SLOT), tok_t(MLA_WIDTH),
                   pl.BlockSpec((None, None, N_HEADS, LANES), lambda b, i: (b, i, 0, 0))],
        out_shape=[jax.ShapeDtypeStruct((B, S, CONV_WIDTH), f32),
                   jax.ShapeDtypeStruct((B, S, 2 * D), bf),
                   jax.ShapeDtypeStruct((B, N_HEADS * K_SLOT, S), bf),
                   jax.ShapeDtypeStruct((B, S, N_HEADS * K_SLOT), bf),
                   jax.ShapeDtypeStruct((B, MLA_WIDTH, S), bf),
                   jax.ShapeDtypeStruct((B, n_t, N_HEADS, LANES), f32)],
        compiler_params=params(dimension_semantics=("arbitrary", "arbitrary")),
        name="proj",
    )(x, positions.reshape(B, 1, S), row(norm1_g), w_lead, wkrT, wg, row(q_norm_g),
      row(kv_norm_g), wuqT, wukp, wuvT, inv_freq.reshape(HALF_ROPE, 1))

    hb = TM // CONV_HALO
    n_hb = S // CONV_HALO
    c = pl.pallas_call(
        _conv_kernel,
        grid=(B, n_t),
        in_specs=[tok(CONV_WIDTH),
                  pl.BlockSpec((None, CONV_HALO, CONV_WIDTH),
                               lambda b, i: (b, jnp.maximum(i * hb - 1, 0), 0)),
                  pl.BlockSpec((None, CONV_HALO, CONV_WIDTH),
                               lambda b, i: (b, jnp.minimum((i + 1) * hb, n_hb - 1), 0)),
                  tok(D),
                  _resident((CONV_KERNEL, CONV_WIDTH)), _resident((1, CONV_WIDTH)),
                  _resident((1, CONV_WIDTH)), _resident((1, CONV_WIDTH)),
                  _resident((CONV_WIDTH, D)), _resident((1, D))],
        out_specs=tok(D),
        out_shape=jax.ShapeDtypeStruct((B, S, D), bf),
        scratch_shapes=[pltpu.VMEM((CONV_WIDTH // LANES, TM + 2 * CONV_HALO, LANES), f32),
                        pltpu.VMEM((CONV_WIDTH // LANES, TM, LANES), f32)],
        compiler_params=params(dimension_semantics=("arbitrary", "arbitrary")),
        name="conv",
    )(u, u, u, g, conv_dw_w.astype(f32), row(conv_dw_b), row(conv_ln_g), row(conv_ln_b),
      w_conv_out.astype(bf), row(b_conv_out))

    n_hp = N_HEADS // HEADS_PER_STEP
    o = pl.pallas_call(
        _attn_kernel,
        grid=(B, n_hp, S // TQ),
        in_specs=[pl.BlockSpec((None, HEADS_PER_STEP * K_SLOT, TQ), lambda b, h, i: (b, h, i)),
                  pl.BlockSpec((None, S, HEADS_PER_STEP * K_SLOT), lambda b, h, i: (b, 0, h)),
                  pl.BlockSpec((None, HEADS_PER_STEP * V_HEAD_DIM, S), lambda b, h, i: (b, h, 0)),
                  pl.BlockSpec((None, n_t, N_HEADS, LANES), lambda b, h, i: (b, 0, 0, 0))],
        out_specs=pl.BlockSpec((None, TQ, HEADS_PER_STEP * V_HEAD_DIM), lambda b, h, i: (b, i, h)),
        out_shape=jax.ShapeDtypeStruct((B, S, MLA_WIDTH), bf),
        scratch_shapes=[
            pltpu.VMEM((HEADS_PER_STEP * K_SLOT, TQ), bf),
            pltpu.VMEM((2, HEADS_PER_STEP * (TQ // TQ_SUB), TK, TQ_SUB), bf),
            pltpu.VMEM((2, HEADS_PER_STEP * (TQ // TQ_SUB), TK, TQ_SUB), f32),
            pltpu.VMEM((HEADS_PER_STEP * (TQ // TQ_SUB), V_HEAD_DIM + 16, TQ_SUB), f32)],
        compiler_params=params(dimension_semantics=("arbitrary", "arbitrary", "arbitrary")),
        name="attn",
    )(q_t, k, v_t, kstat)

    mtok = lambda w: pl.BlockSpec((None, MERGE_TM, w), lambda b, i: (b, i, 0))
    x1, h2 = pl.pallas_call(
        _merge_kernel,
        grid=(B, S // MERGE_TM),
        in_specs=[mtok(MLA_WIDTH), mtok(D),
                  pl.BlockSpec((None, MERGE_TM, D), lambda b, i: (b, i, 1)),
                  mtok(D), _resident((MLA_WIDTH, D)), _resident((D, D)), _resident((1, D))],
        out_specs=[mtok(D), mtok(D)],
        out_shape=[jax.ShapeDtypeStruct((B, S, D), f32), jax.ShapeDtypeStruct((B, S, D), bf)],
        compiler_params=params(dimension_semantics=("arbitrary", "arbitrary")),
        name="merge",
    )(o, c, g, x, w_mla_out.astype(bf), w_out.astype(bf), row(norm2_g))

    fb = TM // FFN_HALO
    n_fb = S // FFN_HALO
    out = pl.pallas_call(
        _ffn_kernel,
        grid=(B, n_t),
        in_specs=[tok(D),
                  pl.BlockSpec((None, FFN_HALO, D), lambda b, i: (b, jnp.maximum(i * fb - 1, 0), 0)),
                  pl.BlockSpec((None, FFN_HALO, D),
                               lambda b, i: (b, jnp.minimum((i + 1) * fb, n_fb - 1), 0)),
                  tok(D), _resident((D, 2 * D_FF)), _resident((FFN_KERNEL, 2 * D_FF)),
                  _resident((1, 2 * D_FF)), _resident((D_FF, D)), _resident((1, D))],
        out_specs=tok(D),
        out_shape=jax.ShapeDtypeStruct((B, S, D), f32),
        scratch_shapes=[pltpu.VMEM((TM + 2 * FFN_HALO, D), bf),
                        pltpu.VMEM((2, 2, FFN_CHUNK // LANES, TM + 2 * FFN_HALO, LANES), f32),
                        pltpu.VMEM((TM, D_FF), bf),
                        pltpu.VMEM((D // LANES, TM, LANES), f32)],
        compiler_params=params(dimension_semantics=("arbitrary", "arbitrary")),
        name="ffn",
    )(h2, h2, h2, x1, w_ffn_up.astype(bf), ffn_dw_w.astype(f32), row(ffn_dw_b),
      w_ffn_down.astype(bf), row(norm_f_g))
    return out


def kernel(x, positions, norm1_g, w_in, conv_dw_w, conv_dw_b, conv_ln_g, conv_ln_b, w_conv_out,
           b_conv_out, q_norm_g, w_uq, kv_norm_g, w_ukv, w_mla_out, w_out, norm2_g, w_ffn_up,
           ffn_dw_w, ffn_dw_b, w_ffn_down, norm_f_g):
    assert norm1_g.shape[0] == 1, "single-layer block"
    return _layer(x, positions, norm1_g[0], w_in[0], conv_dw_w[0], conv_dw_b[0], conv_ln_g[0],
                  conv_ln_b[0], w_conv_out[0], b_conv_out[0], q_norm_g[0], w_uq[0], kv_norm_g[0],
                  w_ukv[0], w_mla_out[0], w_out[0], norm2_g[0], w_ffn_up[0], ffn_dw_w[0],
                  ffn_dw_b[0], w_ffn_down[0], norm_f_g)
```

```python
import functools
import math

import jax
import jax.numpy as jnp
from jax import lax
from jax.experimental import pallas as pl
from jax.experimental.pallas import tpu as pltpu

D_MODEL = 1024
CONV_WIDTH = 512
CONV_KERNEL = 31
N_HEADS = 8
QK_NOPE_DIM = 64
QK_ROPE_DIM = 32
V_HEAD_DIM = 64
Q_LORA_RANK = 384
KV_LORA_RANK = 256
ROPE_THETA = 10000.0
QK_HEAD_DIM = QK_NOPE_DIM + QK_ROPE_DIM
MLA_WIDTH = N_HEADS * V_HEAD_DIM
D_FF = 2816
FFN_KERNEL = 3
NORM_EPS = 1e-6

LANES = 128
HALF_ROPE = QK_ROPE_DIM // 2
K_SLOT = LANES
HEADS_PER_STEP = 2
VMEM_LIMIT = 56 * 1024 * 1024

TM = 512
MERGE_TM = 1024
CONV_HALO = 16
CONV_ROWS = 128
FFN_HALO = 16
FFN_CHUNK = 256
TQ = 1024
TQ_SUB = 256
TK = 256
ATTN_UNROLL = 4
BOUNDED_UNROLL = 16
BOUND_SLACK = 1.01
MIN_TRUSTED_DENOM = 2.0 ** -60

_NT = (((1,), (1,)), ((), ()))


def _rms(x, g):
    return x * lax.rsqrt(jnp.mean(x * x, axis=-1, keepdims=True) + NORM_EPS) * g


def _dot(a, b):
    return jnp.dot(a, b, preferred_element_type=jnp.float32)


def _dot_nt(a, b):
    return lax.dot_general(a, b, _NT, preferred_element_type=jnp.float32)


def _resident(shape):
    nd = len(shape)
    return pl.BlockSpec(shape, lambda *_: (0,) * nd, pipeline_mode=pl.Buffered(1))


def _proj_kernel(x_ref, pos_ref, g1_ref, win_ref, wkrT_ref, wg_ref,
                 qng_ref, kvng_ref, wuqT_ref, wukp_ref, wuvT_ref, invf_ref,
                 u_ref, g_ref, qT_ref, k_ref, vT_ref, kstat_ref, *, q_scale):
    bf = jnp.bfloat16
    hb = _rms(x_ref[...], g1_ref[...]).astype(bf)

    o_q = 2 * CONV_WIDTH
    o_kv = o_q + Q_LORA_RANK
    o_kr = o_kv + KV_LORA_RANK
    qn = _rms(_dot(hb, win_ref[:, o_q:o_kv]), qng_ref[...]).astype(bf)
    kvn = _rms(_dot(hb, win_ref[:, o_kv:o_kr]), kvng_ref[...]).astype(bf)

    ang = invf_ref[...] * pos_ref[...].astype(jnp.float32)
    cos_t, sin_t = jnp.cos(ang), jnp.sin(ang)

    def rope_t(t1, t2):
        return t1 * cos_t - t2 * sin_t, t2 * cos_t + t1 * sin_t

    q_t = _dot_nt(wuqT_ref[...], qn)
    tm = q_t.shape[1]
    slot_pad = jnp.zeros((K_SLOT - QK_HEAD_DIM, tm), jnp.float32)
    for h in range(N_HEADS):
        r0 = h * QK_HEAD_DIM
        r1, r2 = rope_t(q_t[r0 + QK_NOPE_DIM:r0 + QK_NOPE_DIM + HALF_ROPE],
                        q_t[r0 + QK_NOPE_DIM + HALF_ROPE:r0 + QK_HEAD_DIM])
        blk = jnp.concatenate([q_t[r0:r0 + QK_NOPE_DIM] * q_scale, r1 * q_scale, r2 * q_scale, slot_pad],
                              axis=0)
        qT_ref[h * K_SLOT:(h + 1) * K_SLOT, :] = blk.astype(bf)

    kr_t = _dot_nt(wkrT_ref[...], hb)
    k1, k2 = rope_t(kr_t[:HALF_ROPE], kr_t[HALF_ROPE:])
    tail_rows = lax.broadcasted_iota(jnp.int32, (K_SLOT - QK_HEAD_DIM, tm), 0)
    kr_slot_t = jnp.concatenate(
        [jnp.zeros((QK_NOPE_DIM, tm), jnp.float32), k1, k2,
         jnp.where(tail_rows == 0, 1.0, 0.0)], axis=0)
    kr_slot = kr_slot_t.T
    knp = _dot(kvn, wukp_ref[...])
    head_rows = lax.broadcasted_iota(jnp.int32, (N_HEADS, LANES), 0)
    kstat = jnp.zeros((N_HEADS, LANES), jnp.float32)
    for h in range(N_HEADS):
        c0 = h * K_SLOT
        k_h = knp[:, c0:c0 + K_SLOT] + kr_slot
        k_ref[:, c0:c0 + K_SLOT] = k_h.astype(bf)
        norm2 = jnp.max(jnp.sum(k_h * k_h, axis=-1, keepdims=True), axis=0, keepdims=True)
        kstat = jnp.where(head_rows == h, norm2, kstat)
    kstat_ref[...] = kstat

    vT_ref[...] = _dot_nt(wuvT_ref[...], kvn).astype(bf)

    a = _dot(hb, win_ref[:, :o_q])
    u_ref[...] = a[:, :CONV_WIDTH] * jax.nn.sigmoid(a[:, CONV_WIDTH:])
    g_ref[...] = jax.nn.sigmoid(_dot(hb, wg_ref[...])).astype(bf)


def _conv_kernel(u_ref, up_ref, un_ref, g_ref, dww_ref, dwb_ref, lng_ref, lnb_ref,
                 wo_ref, bo_ref, c_ref, ext_ref, act_ref):
    i = pl.program_id(1)
    n = pl.num_programs(1)
    tm = u_ref.shape[0]
    n_lt = CONV_WIDTH // LANES
    lane = lambda l: slice(l * LANES, (l + 1) * LANES)
    for l in range(n_lt):
        ext_ref[l, 0:CONV_HALO, :] = jnp.where(i > 0, up_ref[:, lane(l)], 0.0)
        ext_ref[l, CONV_HALO:CONV_HALO + tm, :] = u_ref[:, lane(l)]
        ext_ref[l, CONV_HALO + tm:, :] = jnp.where(i < n - 1, un_ref[:, lane(l)], 0.0)

    half = CONV_ROWS // 2
    first = CONV_HALO - CONV_KERNEL // 2

    def rows_chunk(ci):
        r0 = ci * CONV_ROWS
        for par in range(2):
            ys = []
            for l in range(n_lt):
                acc = jnp.zeros((half, LANES), jnp.float32)
                for k in range(CONV_KERNEL):
                    tap = ext_ref[l, pl.ds(first + r0 + par + k, half, stride=2), :]
                    acc = acc + tap * dww_ref[k:k + 1, lane(l)]
                ys.append(acc)
            y = jnp.concatenate(ys, axis=1) + dwb_ref[...]
            mu = jnp.mean(y, axis=-1, keepdims=True)
            yc = y - mu
            yn = yc * lax.rsqrt(jnp.mean(yc * yc, axis=-1, keepdims=True) + NORM_EPS)
            yn = yn * lng_ref[...] + lnb_ref[...]
            a = yn * jax.nn.sigmoid(yn)
            for l in range(n_lt):
                act_ref[l, pl.ds(r0 + par, half, stride=2), :] = a[:, lane(l)]

    for ci in range(tm // CONV_ROWS):
        rows_chunk(ci)
    act = jnp.concatenate([act_ref[l] for l in range(n_lt)], axis=1).astype(jnp.bfloat16)
    y_conv = _dot(act, wo_ref[...]) + bo_ref[...]
    c_ref[...] = (g_ref[...].astype(jnp.float32) * y_conv).astype(jnp.bfloat16)


def _attn_chains(tq):
    n_sub = tq // TQ_SUB
    return [(h, qs) for h in range(HEADS_PER_STEP) for qs in range(n_sub)]


def _value_rows(vT_ref, h, r0):
    return jnp.concatenate([vT_ref[h * V_HEAD_DIM:(h + 1) * V_HEAD_DIM, pl.ds(r0, TK)],
                            jnp.ones((16, TK), jnp.bfloat16)], axis=0)


def _attn_running_max(qT_ref, k_ref, vT_ref, s_buf, acc_ref):
    bf = jnp.bfloat16
    n_chunks = k_ref.shape[0] // TK
    chains = _attn_chains(qT_ref.shape[1])

    def scores(j, slot):
        r0 = j * TK if isinstance(j, int) else pl.multiple_of(j * TK, TK)
        tops = []
        for c, (h, qs) in enumerate(chains):
            k_blk = k_ref[pl.ds(r0, TK), h * K_SLOT:(h + 1) * K_SLOT]
            q_t = qT_ref[h * K_SLOT:(h + 1) * K_SLOT, qs * TQ_SUB:(qs + 1) * TQ_SUB]
            s_t = _dot(k_blk, q_t)
            s_buf[slot, c] = s_t
            tops.append(jnp.max(s_t, axis=0, keepdims=True))
        return tuple(tops)

    def consume(j, slot, ms, tops):
        r0 = j * TK if isinstance(j, int) else pl.multiple_of(j * TK, TK)
        out = []
        for c, (h, qs) in enumerate(chains):
            m_new = jnp.maximum(ms[c], tops[c])
            p_t = jnp.exp2(s_buf[slot, c] - m_new).astype(bf)
            acc_ref[c] = jnp.exp2(ms[c] - m_new) * acc_ref[c] + _dot(_value_rows(vT_ref, h, r0), p_t)
            out.append(m_new)
        return tuple(out)

    acc_ref[...] = jnp.zeros_like(acc_ref)
    ms = tuple(jnp.full((1, TQ_SUB), -jnp.inf, jnp.float32) for _ in chains)
    tops = scores(0, 0)

    def group(first, carry, last=False):
        ms, tops = carry
        for u in range(ATTN_UNROLL):
            nxt = None if (last and u == ATTN_UNROLL - 1) else scores(first + u + 1, (u + 1) % 2)
            ms = consume(first + u, u % 2, ms, tops)
            tops = nxt
        return ms, tops

    n_groups = n_chunks // ATTN_UNROLL
    carry = lax.fori_loop(0, n_groups - 1, lambda i, cr: group(i * ATTN_UNROLL, cr), (ms, tops))
    group((n_groups - 1) * ATTN_UNROLL, carry, last=True)


def _attn_bounded(q_aug, k_ref, vT_ref, p_buf, acc_ref):
    n_chunks = k_ref.shape[0] // TK
    chains = _attn_chains(q_aug.shape[1])
    row0 = lambda j: j * TK if isinstance(j, int) else pl.multiple_of(j * TK, TK)

    def produce(j, slot, c):
        h, qs = chains[c]
        k_blk = k_ref[pl.ds(row0(j), TK), h * K_SLOT:(h + 1) * K_SLOT]
        q_t = q_aug[h * K_SLOT:(h + 1) * K_SLOT, qs * TQ_SUB:(qs + 1) * TQ_SUB]
        p_t = jnp.exp2(_dot(k_blk, q_t))
        p_buf[slot, c] = p_t.astype(jnp.bfloat16)
        acc_ref[c, V_HEAD_DIM:V_HEAD_DIM + 1, :] += jnp.sum(p_t, axis=0, keepdims=True)

    def consume(j, slot, c):
        h, _ = chains[c]
        v_t = vT_ref[h * V_HEAD_DIM:(h + 1) * V_HEAD_DIM, pl.ds(row0(j), TK)]
        acc_ref[c, 0:V_HEAD_DIM, :] += _dot(v_t, p_buf[slot, c])

    def group(first, last=False):
        for u in range(BOUNDED_UNROLL):
            for c in range(len(chains)):
                if not (last and u == BOUNDED_UNROLL - 1):
                    produce(first + u + 1, (u + 1) % 2, c)
                consume(first + u, u % 2, c)

    acc_ref[...] = jnp.zeros_like(acc_ref)
    for c in range(len(chains)):
        produce(0, 0, c)
    n_groups = n_chunks // BOUNDED_UNROLL

    def body(i, carry):
        group(i * BOUNDED_UNROLL)
        return carry

    lax.fori_loop(0, n_groups - 1, body, 0)
    group((n_groups - 1) * BOUNDED_UNROLL, last=True)


def _attn_kernel(qT_ref, k_ref, vT_ref, kstat_ref, o_ref, q_aug, p_buf, s_buf, acc_ref):
    f32 = jnp.float32
    tq = qT_ref.shape[1]
    n_sub = tq // TQ_SUB
    hp = pl.program_id(1)
    key_norm2 = jnp.max(kstat_ref[...], axis=0)
    head_rows = lax.broadcasted_iota(jnp.int32, key_norm2.shape, 0)
    q_rows = lax.broadcasted_iota(jnp.int32, (K_SLOT, tq), 0)
    for h in range(HEADS_PER_STEP):
        q = qT_ref[h * K_SLOT:(h + 1) * K_SLOT, :].astype(f32)
        k2 = jnp.max(jnp.where(head_rows == HEADS_PER_STEP * hp + h, key_norm2, 0.0),
                     axis=0, keepdims=True)
        k2 = jnp.concatenate([k2] * (tq // LANES), axis=1)
        bound = jnp.sqrt(jnp.sum(q * q, axis=0, keepdims=True) * k2) * BOUND_SLACK
        q_aug[h * K_SLOT:(h + 1) * K_SLOT, :] = jnp.where(
            q_rows == QK_HEAD_DIM, -bound, q).astype(jnp.bfloat16)

    _attn_bounded(q_aug, k_ref, vT_ref, p_buf, acc_ref)
    denom_min = jnp.min(acc_ref[:, V_HEAD_DIM:V_HEAD_DIM + 1, :])
    trusted = denom_min > MIN_TRUSTED_DENOM

    @pl.when(jnp.logical_not(trusted))
    def _():
        _attn_running_max(qT_ref, k_ref, vT_ref, s_buf, acc_ref)

    for qs in range(n_sub):
        o_t = jnp.concatenate(
            [acc_ref[h * n_sub + qs, :V_HEAD_DIM] / acc_ref[h * n_sub + qs, V_HEAD_DIM:V_HEAD_DIM + 1]
             for h in range(HEADS_PER_STEP)], axis=0)
        o_ref[qs * TQ_SUB:(qs + 1) * TQ_SUB, :] = o_t.T.astype(jnp.bfloat16)


def _merge_kernel(o_ref, c_ref, g_ref, x_ref, wm_ref, wo_ref, n2_ref, x1_ref, h2_ref):
    y_mla = _dot(o_ref[...], wm_ref[...])
    merged = c_ref[...].astype(jnp.float32) + g_ref[...].astype(jnp.float32) * y_mla
    x1 = x_ref[...] + _dot(merged.astype(jnp.bfloat16), wo_ref[...])
    x1_ref[...] = x1
    h2_ref[...] = _rms(x1, n2_ref[...]).astype(jnp.bfloat16)


def _ffn_kernel(h_ref, hp_ref, hn_ref, x1_ref, wup_ref, dww_ref, dwb_ref, wdn_ref, nf_ref,
                out_ref, hext_ref, z_buf, act_ref, y_ref):
    i = pl.program_id(1)
    n = pl.num_programs(1)
    tm = h_ref.shape[0]
    half = tm // 2
    n_c = D_FF // FFN_CHUNK
    n_lt = FFN_CHUNK // LANES
    lane = lambda l: slice(l * LANES, (l + 1) * LANES)
    hext_ref[0:FFN_HALO, :] = jnp.where(i > 0, hp_ref[...], jnp.zeros_like(hp_ref))
    hext_ref[FFN_HALO:FFN_HALO + tm, :] = h_ref[...]
    hext_ref[FFN_HALO + tm:, :] = jnp.where(i < n - 1, hn_ref[...], jnp.zeros_like(hn_ref))

    def up_proj(c):
        for part in range(2):
            c0 = part * D_FF + c * FFN_CHUNK
            z = _dot(hext_ref[...], wup_ref[:, c0:c0 + FFN_CHUNK])
            for l in range(n_lt):
                z_buf[c % 2, part, l] = z[:, lane(l)]

    def conv3(c, part, par):
        ys = []
        for l in range(n_lt):
            c0 = part * D_FF + c * FFN_CHUNK + l * LANES
            w = dww_ref[:, c0:c0 + LANES]
            taps = [z_buf[c % 2, part, l, pl.ds(FFN_HALO + par + k - 1, half, stride=2), :]
                    for k in range(FFN_KERNEL)]
            ys.append(taps[0] * w[0:1] + taps[1] * w[1:2] + taps[2] * w[2:3]
                      + dwb_ref[:, c0:c0 + LANES])
        return jnp.concatenate(ys, axis=1)

    def activate(c):
        for par in range(2):
            gate, up = conv3(c, 0, par), conv3(c, 1, par)
            act_ref[par * half:(par + 1) * half, c * FFN_CHUNK:(c + 1) * FFN_CHUNK] = (
                gate * jax.nn.sigmoid(gate) * up).astype(jnp.bfloat16)

    def down_proj(c):
        return _dot(act_ref[:, c * FFN_CHUNK:(c + 1) * FFN_CHUNK],
                    wdn_ref[c * FFN_CHUNK:(c + 1) * FFN_CHUNK, :])

    up_proj(0)
    y = None
    for c in range(n_c):
        if c + 1 < n_c:
            up_proj(c + 1)
        if c >= 1:
            y = down_proj(c - 1) if y is None else y + down_proj(c - 1)
        activate(c)
    y = y + down_proj(n_c - 1)
    for par in range(2):
        for l in range(D_MODEL // LANES):
            y_ref[l, pl.ds(par, half, stride=2), :] = y[par * half:(par + 1) * half, lane(l)]
    y_tok = jnp.concatenate([y_ref[l] for l in range(D_MODEL // LANES)], axis=1)
    out_ref[...] = _rms(x1_ref[...] + y_tok, nf_ref[...])


def _layer(x, positions, norm1_g, w_in, conv_dw_w, conv_dw_b, conv_ln_g, conv_ln_b,
           w_conv_out, b_conv_out, q_norm_g, w_uq, kv_norm_g, w_ukv, w_mla_out, w_out,
           norm2_g, w_ffn_up, ffn_dw_w, ffn_dw_b, w_ffn_down, norm_f_g):
    bf = jnp.bfloat16
    f32 = jnp.float32
    B, S, D = x.shape
    assert D == D_MODEL and S % TM == 0 and S % MERGE_TM == 0 and S % TQ == 0
    assert (S // TK) % ATTN_UNROLL == 0 and (S // TK) % BOUNDED_UNROLL == 0 and TQ % TQ_SUB == 0
    n_t = S // TM
    row = lambda v: v.reshape(1, -1).astype(f32)
    params = functools.partial(pltpu.CompilerParams, vmem_limit_bytes=VMEM_LIMIT)

    o_q = 2 * CONV_WIDTH
    o_kv = o_q + Q_LORA_RANK
    o_kr = o_kv + KV_LORA_RANK
    o_g = o_kr + QK_ROPE_DIM
    w_lead = w_in[:, :o_kr].astype(bf)
    wkrT = w_in[:, o_kr:o_g].T.astype(bf)
    wg = w_in[:, o_g:].astype(bf)
    wuqT = w_uq.T.astype(bf)
    w_ukv_h = w_ukv.reshape(KV_LORA_RANK, N_HEADS, QK_NOPE_DIM + V_HEAD_DIM)
    wukp = jnp.pad(w_ukv_h[:, :, :QK_NOPE_DIM], ((0, 0), (0, 0), (0, K_SLOT - QK_NOPE_DIM)))
    wukp = wukp.reshape(KV_LORA_RANK, N_HEADS * K_SLOT).astype(bf)
    wuvT = w_ukv_h[:, :, QK_NOPE_DIM:].reshape(KV_LORA_RANK, MLA_WIDTH).T.astype(bf)
    inv_freq = 1.0 / (ROPE_THETA ** (jnp.arange(0, QK_ROPE_DIM, 2, dtype=f32) / QK_ROPE_DIM))
    q_scale = (QK_HEAD_DIM ** -0.5) * math.log2(math.e)

    tok = lambda w: pl.BlockSpec((None, TM, w), lambda b, i: (b, i, 0))
    tok_t = lambda r: pl.BlockSpec((None, r, TM), lambda b, i: (b, 0, i))

    u, g, q_t, k, v_t, kstat = pl.pallas_call(
        functools.partial(_proj_kernel, q_scale=q_scale),
        grid=(B, n_t),
        in_specs=[tok(D), tok_t(1),
                  _resident((1, D)), _resident(w_lead.shape), _resident(wkrT.shape), _resident(wg.shape),
                  _resident((1, Q_LORA_RANK)), _resident((1, KV_LORA_RANK)), _resident(wuqT.shape),
                  _resident(wukp.shape), _resident(wuvT.shape), _resident((HALF_ROPE, 1))],
        out_specs=[tok(CONV_WIDTH), tok(2 * D), tok_t(N_HEADS * K_SLOT),
                   tok(N_HEADS * K_SLOT), tok_t(MLA_WIDTH),
                   pl.BlockSpec((None, None, N_HEADS, LANES), lambda b, i: (b, i, 0, 0))],
        out_shape=[jax.ShapeDtypeStruct((B, S, CONV_WIDTH), f32),
                   jax.ShapeDtypeStruct((B, S, 2 * D), bf),
                   jax.ShapeDtypeStruct((B, N_HEADS * K_SLOT, S), bf),
                   jax.ShapeDtypeStruct((B, S, N_HEADS * K_SLOT), bf),
                   jax.ShapeDtypeStruct((B, MLA_WIDTH, S), bf),
                   jax.ShapeDtypeStruct((B, n_t, N_HEADS, LANES), f32)],
        compiler_params=params(dimension_semantics=("arbitrary", "arbitrary")),
        name="proj",
    )(x, positions.reshape(B, 1, S), row(norm1_g), w_lead, wkrT, wg, row(q_norm_g),
      row(kv_norm_g), wuqT, wukp, wuvT, inv_freq.reshape(HALF_ROPE, 1))

    hb = TM // CONV_HALO
    n_hb = S // CONV_HALO
    c = pl.pallas_call(
        _conv_kernel,
        grid=(B, n_t),
        in_specs=[tok(CONV_WIDTH),
                  pl.BlockSpec((None, CONV_HALO, CONV_WIDTH),
                               lambda b, i: (b, jnp.maximum(i * hb - 1, 0), 0)),
                  pl.BlockSpec((None, CONV_HALO, CONV_WIDTH),
                               lambda b, i: (b, jnp.minimum((i + 1) * hb, n_hb - 1), 0)),
                  tok(D),
                  _resident((CONV_KERNEL, CONV_WIDTH)), _resident((1, CONV_WIDTH)),
                  _resident((1, CONV_WIDTH)), _resident((1, CONV_WIDTH)),
                  _resident((CONV_WIDTH, D)), _resident((1, D))],
        out_specs=tok(D),
        out_shape=jax.ShapeDtypeStruct((B, S, D), bf),
        scratch_shapes=[pltpu.VMEM((CONV_WIDTH // LANES, TM + 2 * CONV_HALO, LANES), f32),
                        pltpu.VMEM((CONV_WIDTH // LANES, TM, LANES), f32)],
        compiler_params=params(dimension_semantics=("arbitrary", "arbitrary")),
        name="conv",
    )(u, u, u, g, conv_dw_w.astype(f32), row(conv_dw_b), row(conv_ln_g), row(conv_ln_b),
      w_conv_out.astype(bf), row(b_conv_out))

    n_hp = N_HEADS // HEADS_PER_STEP
    o = pl.pallas_call(
        _attn_kernel,
        grid=(B, n_hp, S // TQ),
        in_specs=[pl.BlockSpec((None, HEADS_PER_STEP * K_SLOT, TQ), lambda b, h, i: (b, h, i)),
                  pl.BlockSpec((None, S, HEADS_PER_STEP * K_SLOT), lambda b, h, i: (b, 0, h)),
                  pl.BlockSpec((None, HEADS_PER_STEP * V_HEAD_DIM, S), lambda b, h, i: (b, h, 0)),
                  pl.BlockSpec((None, n_t, N_HEADS, LANES), lambda b, h, i: (b, 0, 0, 0))],
        out_specs=pl.BlockSpec((None, TQ, HEADS_PER_STEP * V_HEAD_DIM), lambda b, h, i: (b, i, h)),
        out_shape=jax.ShapeDtypeStruct((B, S, MLA_WIDTH), bf),
        scratch_shapes=[
            pltpu.VMEM((HEADS_PER_STEP * K_SLOT, TQ), bf),
            pltpu.VMEM((2, HEADS_PER_STEP * (TQ // TQ_SUB), TK, TQ_SUB), bf),
            pltpu.VMEM((2, HEADS_PER_STEP * (TQ // TQ_SUB), TK, TQ_SUB), f32),
            pltpu.VMEM((HEADS_PER_STEP * (TQ // TQ_SUB), V_HEAD_DIM + 16, TQ_SUB), f32)],
        compiler_params=params(dimension_semantics=("arbitrary", "arbitrary", "arbitrary")),
        name="attn",
    )(q_t, k, v_t, kstat)

    mtok = lambda w: pl.BlockSpec((None, MERGE_TM, w), lambda b, i: (b, i, 0))
    x1, h2 = pl.pallas_call(
        _merge_kernel,
        grid=(B, S // MERGE_TM),
        in_specs=[mtok(MLA_WIDTH), mtok(D),
                  pl.BlockSpec((None, MERGE_TM, D), lambda b, i: (b, i, 1)),
                  mtok(D), _resident((MLA_WIDTH, D)), _resident((D, D)), _resident((1, D))],
        out_specs=[mtok(D), mtok(D)],
        out_shape=[jax.ShapeDtypeStruct((B, S, D), f32), jax.ShapeDtypeStruct((B, S, D), bf)],
        compiler_params=params(dimension_semantics=("arbitrary", "arbitrary")),
        name="merge",
    )(o, c, g, x, w_mla_out.astype(bf), w_out.astype(bf), row(norm2_g))

    fb = TM // FFN_HALO
    n_fb = S // FFN_HALO
    out = pl.pallas_call(
        _ffn_kernel,
        grid=(B, n_t),
        in_specs=[tok(D),
                  pl.BlockSpec((None, FFN_HALO, D), lambda b, i: (b, jnp.maximum(i * fb - 1, 0), 0)),
                  pl.BlockSpec((None, FFN_HALO, D),
                               lambda b, i: (b, jnp.minimum((i + 1) * fb, n_fb - 1), 0)),
                  tok(D), _resident((D, 2 * D_FF)), _resident((FFN_KERNEL, 2 * D_FF)),
                  _resident((1, 2 * D_FF)), _resident((D_FF, D)), _resident((1, D))],
        out_specs=tok(D),
        out_shape=jax.ShapeDtypeStruct((B, S, D), f32),
        scratch_shapes=[pltpu.VMEM((TM + 2 * FFN_HALO, D), bf),
                        pltpu.VMEM((2, 2, FFN_CHUNK // LANES, TM + 2 * FFN_HALO, LANES), f32),
                        pltpu.VMEM((TM, D_FF), bf),
                        pltpu.VMEM((D // LANES, TM, LANES), f32)],
        compiler_params=params(dimension_semantics=("arbitrary", "arbitrary")),
        name="ffn",
    )(h2, h2, h2, x1, w_ffn_up.astype(bf), ffn_dw_w.astype(f32), row(ffn_dw_b),
      w_ffn_down.astype(bf), row(norm_f_g))
    return out


def kernel(x, positions, norm1_g, w_in, conv_dw_w, conv_dw_b, conv_ln_g, conv_ln_b, w_conv_out,
           b_conv_out, q_norm_g, w_uq, kv_norm_g, w_ukv, w_mla_out, w_out, norm2_g, w_ffn_up,
           ffn_dw_w, ffn_dw_b, w_ffn_down, norm_f_g):
    assert norm1_g.shape[0] == 1, "single-layer block"
    return _layer(x, positions, norm1_g[0], w_in[0], conv_dw_w[0], conv_dw_b[0], conv_ln_g[0],
                  conv_ln_b[0], w_conv_out[0], b_conv_out[0], q_norm_g[0], w_uq[0], kv_norm_g[0],
                  w_ukv[0], w_mla_out[0], w_out[0], norm2_g[0], w_ffn_up[0], ffn_dw_w[0],
                  ffn_dw_b[0], w_ffn_down[0], norm_f_g)
```
